```python
import math
import jax
import jax.numpy as jnp
from jax import lax
import numpy as np


D_MODEL = 1024
BATCH = 8
SEQ = 4096
DEPTH = 4

MLA_HEADS = 8
MLA_NOPE = 64
MLA_ROPE = 32
MLA_V = 64
Q_LORA = 384
KV_LORA = 256
ROPE_THETA = 10000.0
Q_BLOCK = 128
NEG_INF = -1e30
RWKV_HEADS = 8
RWKV_HEAD = 64
RWKV_WIDTH = RWKV_HEADS * RWKV_HEAD
DECAY_LORA = 64
ICLR_LORA = 64
VRES_LORA = 32
GATE_LORA = 128
RWKV_GN_EPS = 64e-5
S5_GROUP = 16
S5_GROUPS = 32
S5_WIDTH = S5_GROUP * S5_GROUPS
S5_STATE = 64
N_BRANCH = 3
BRANCH_WIDTH = 512
MLA_IN = Q_LORA + KV_LORA + MLA_ROPE
RWKV_IN = 3 * RWKV_WIDTH + DECAY_LORA + ICLR_LORA + GATE_LORA
GATE_IN = N_BRANCH * D_MODEL
IN_WIDTH = MLA_IN + RWKV_IN + S5_WIDTH + GATE_IN
FFN_DIM = 2816
N_EXPERTS = 8
TOP_K = 2
EXPERT_DIM = 3584
MOE_BLOCK = 512
DEEPNORM_ALPHA = (2 * DEPTH) ** 0.25
DEEPNORM_BETA = (8 * DEPTH) ** -0.25
MAX_POS_OFFSET = 4096

kernel_name = 'hybrid_mla_rwkv7_s5_moe_trunk'


def _layer_norm(x, g, b, eps=1e-5):
    xf = x.astype(jnp.float32)
    xc = xf - jnp.mean(xf, axis=-1, keepdims=True)
    var = jnp.mean(xc * xc, axis=-1, keepdims=True)
    return (xc * lax.rsqrt(var + eps) * g.astype(jnp.float32) + b.astype(jnp.float32)).astype(x.dtype)


def _rms_norm(x, g, eps=1e-6):
    xf = x.astype(jnp.float32)
    return (xf * lax.rsqrt(jnp.mean(xf * xf, axis=-1, keepdims=True) + eps) * g.astype(jnp.float32)).astype(x.dtype)


def _rope(x, cos, sin):
    half = x.shape[-1] // 2
    x1, x2 = x[..., :half], x[..., half:]
    return jnp.concatenate([x1 * cos - x2 * sin, x1 * sin + x2 * cos], axis=-1).astype(x.dtype)


def _mla_branch(p, cos, sin, q_norm, w_uq, kv_norm, w_ukv):
    bsz, seq, _ = p.shape
    c_q = p[..., :Q_LORA]
    c_kv = p[..., Q_LORA:Q_LORA + KV_LORA]
    k_pe = _rope(p[..., Q_LORA + KV_LORA:], cos, sin)
    q = (_rms_norm(c_q, q_norm) @ w_uq).reshape(bsz, seq, MLA_HEADS, MLA_NOPE + MLA_ROPE)
    kv = (_rms_norm(c_kv, kv_norm) @ w_ukv).reshape(bsz, seq, MLA_HEADS, MLA_NOPE + MLA_V)
    q_nope = q[..., :MLA_NOPE]
    q_pe = _rope(q[..., MLA_NOPE:], cos[:, :, None], sin[:, :, None])
    k_nope, v = kv[..., :MLA_NOPE], kv[..., MLA_NOPE:]
    scale = (MLA_NOPE + MLA_ROPE) ** -0.5
    outs = []
    for blk in range(seq // Q_BLOCK):
        q0, q1 = blk * Q_BLOCK, (blk + 1) * Q_BLOCK
        s = (jnp.einsum('bqhd,bkhd->bhqk', q_nope[:, q0:q1], k_nope[:, :q1])
             + jnp.einsum('bqhr,bkr->bhqk', q_pe[:, q0:q1], k_pe[:, :q1]))
        s = s.astype(jnp.float32) * scale
        causal = (q0 + jnp.arange(Q_BLOCK))[:, None] >= jnp.arange(q1)[None, :]
        prob = jax.nn.softmax(jnp.where(causal, s, NEG_INF), axis=-1).astype(v.dtype)
        outs.append(jnp.einsum('bhqk,bkhd->bqhd', prob, v[:, :q1]))
    return jnp.concatenate(outs, axis=1).reshape(bsz, seq, MLA_HEADS * MLA_V)


def _rwkv7_step(state, inp):
    r, w, k, v, a, b = inp
    sa = jnp.einsum('bhij,bhj->bhi', state, a)
    state = state * w[:, :, None, :] + sa[..., None] * b[:, :, None, :] + v[..., None] * k[:, :, None, :]
    return state, jnp.einsum('bhij,bhj->bhi', state, r)


def _rwkv7_branch(p, v_first, mu, w0, w_up, a0, a_up, g_up, k_k, k_a, r_k, ln_g, ln_b, vres):
    bsz, seq, _ = p.shape
    prev = jnp.pad(p[:, :-1], ((0, 0), (1, 0), (0, 0)))
    p = p + (prev - p) * mu
    W = RWKV_WIDTH
    r, k, v = p[..., :W], p[..., W:2 * W], p[..., 2 * W:3 * W]
    o = 3 * W
    w_d = p[..., o:o + DECAY_LORA]
    a_d = p[..., o + DECAY_LORA:o + DECAY_LORA + ICLR_LORA]
    g_d = p[..., o + DECAY_LORA + ICLR_LORA:]
    w_log = -jax.nn.softplus(-(w0 + jnp.tanh(w_d) @ w_up).astype(jnp.float32)) - 0.5
    decay = jnp.exp(-jnp.exp(w_log))
    iclr = jax.nn.sigmoid((a0 + a_d @ a_up).astype(jnp.float32))
    g = jax.nn.sigmoid(g_d) @ g_up
    if vres is None:
        v_first = v
    else:
        vd, vu, vb = vres
        v = v + (v_first - v) * jax.nn.sigmoid(vb + (v @ vd) @ vu)
    def heads(t):
        return t.reshape(bsz, seq, RWKV_HEADS, RWKV_HEAD).astype(jnp.float32)
    r_h, v_h, dec_h, iclr_h = heads(r), heads(v), heads(decay), heads(iclr)
    kk = heads(k * k_k)
    kk = kk / jnp.maximum(jnp.sqrt(jnp.sum(kk * kk, axis=-1, keepdims=True)), 1e-12)
    k_a_h = k_a.reshape(RWKV_HEADS, RWKV_HEAD).astype(jnp.float32)
    k_h = heads(k) * (1.0 + (iclr_h - 1.0) * k_a_h)
    def tm(t):
        return jnp.moveaxis(t, 1, 0)
    s0 = jnp.zeros((bsz, RWKV_HEADS, RWKV_HEAD, RWKV_HEAD), jnp.float32)
    _, y = lax.scan(_rwkv7_step, s0, (tm(r_h), tm(dec_h), tm(k_h), tm(v_h), tm(-kk), tm(kk * iclr_h)))
    y = jnp.moveaxis(y, 0, 1)
    yc = y - jnp.mean(y, axis=-1, keepdims=True)
    y = yc * lax.rsqrt(jnp.mean(yc * yc, axis=-1, keepdims=True) + RWKV_GN_EPS)
    y = y.reshape(bsz, seq, W) * ln_g.astype(jnp.float32) + ln_b.astype(jnp.float32)
    bonus = jnp.sum(r_h * k_h * r_k.astype(jnp.float32), axis=-1, keepdims=True) * v_h
    out = (y + bonus.reshape(bsz, seq, W)) * g.astype(jnp.float32)
    return out.astype(p.dtype), v_first


def _complex_linear_combine(e1, e2):
    a1r, a1i, b1r, b1i = e1
    a2r, a2i, b2r, b2i = e2
    return (a1r * a2r - a1i * a2i, a1r * a2i + a1i * a2r,
            a2r * b1r - a2i * b1i + b2r, a2r * b1i + a2i * b1r + b2i)


def _s5_branch(u, a_re, a_im, log_dt, b_re, b_im, c_re, c_im, d, w_glu, b_glu):
    bsz, seq, _ = u.shape
    uf = u.astype(jnp.float32).reshape(bsz, seq, S5_GROUPS, S5_GROUP)
    dt = jnp.exp(log_dt.astype(jnp.float32))[:, None]
    lam_re = jnp.minimum(a_re.astype(jnp.float32), -1e-4)
    lam_im = a_im.astype(jnp.float32)
    mag = jnp.exp(dt * lam_re)
    ab_re, ab_im = mag * jnp.cos(dt * lam_im), mag * jnp.sin(dt * lam_im)
    den = lam_re * lam_re + lam_im * lam_im
    f_re = ((ab_re - 1.0) * lam_re + ab_im * lam_im) / den
    f_im = (ab_im * lam_re - (ab_re - 1.0) * lam_im) / den
    bb_re = f_re[..., None] * b_re - f_im[..., None] * b_im
    bb_im = f_re[..., None] * b_im + f_im[..., None] * b_re
    bu_re = jnp.einsum('bsgc,gpc->bsgp', uf, bb_re)
    bu_im = jnp.einsum('bsgc,gpc->bsgp', uf, bb_im)
    a_seq_re = jnp.broadcast_to(ab_re, (bsz, seq, S5_GROUPS, S5_STATE))
    a_seq_im = jnp.broadcast_to(ab_im, (bsz, seq, S5_GROUPS, S5_STATE))
    _, _, x_re, x_im = lax.associative_scan(_complex_linear_combine, (a_seq_re, a_seq_im, bu_re, bu_im), axis=1)
    y = (jnp.einsum('bsgp,gcp->bsgc', x_re, c_re) - jnp.einsum('bsgp,gcp->bsgc', x_im, c_im)
         + d.astype(jnp.float32) * uf)
    y = jax.nn.gelu(y.reshape(bsz, seq, S5_WIDTH))
    return (y * jax.nn.sigmoid(y @ w_glu + b_glu)).astype(u.dtype)


def _swiglu(h, w_gate, w_up, w_down):
    return (jax.nn.silu(h @ w_gate) * (h @ w_up)) @ w_down


def _moe_swiglu(h, router, w_gate, w_up, w_down):
    bsz, seq, dm = h.shape
    ht = h.reshape(-1, dm)
    n_tok = ht.shape[0]
    n_pair = n_tok * TOP_K
    logits = (ht @ router).astype(jnp.float32)
    top_logit, top_e = lax.top_k(logits, TOP_K)
    top_w = jax.nn.softmax(top_logit, axis=-1)
    pair_e = top_e.reshape(-1)
    pair_tok = jnp.arange(n_pair, dtype=jnp.int32) // TOP_K
    order = jnp.argsort(pair_e)
    se = pair_e[order]
    counts = jnp.bincount(pair_e, length=N_EXPERTS)
    start = jnp.cumsum(counts) - counts
    padded = (counts + MOE_BLOCK - 1) // MOE_BLOCK * MOE_BLOCK
    pad_end = jnp.cumsum(padded)
    pad_start = pad_end - padded
    dest = pad_start[se] + jnp.arange(n_pair, dtype=jnp.int32) - start[se]
    n_blocks = n_pair // MOE_BLOCK + N_EXPERTS
    slot_tok = jnp.zeros((n_blocks * MOE_BLOCK,), jnp.int32).at[dest].set(pair_tok[order])
    slot_w = jnp.zeros((n_blocks * MOE_BLOCK,), jnp.float32).at[dest].set(top_w.reshape(-1)[order])
    block_e = jnp.minimum(jnp.searchsorted(pad_end, jnp.arange(n_blocks, dtype=jnp.int32) * MOE_BLOCK, side='right'), N_EXPERTS - 1)
    def expert_block(args):
        tok, e = args
        xb = ht[tok]
        return (jax.nn.silu(xb @ w_gate[e]) * (xb @ w_up[e])) @ w_down[e]
    y = lax.map(expert_block, (slot_tok.reshape(n_blocks, MOE_BLOCK), block_e))
    y = y.reshape(-1, dm).astype(jnp.float32) * slot_w[:, None]
    out = jax.ops.segment_sum(y, slot_tok, num_segments=n_tok)
    return out.astype(h.dtype).reshape(bsz, seq, dm)


def setup_inputs(seed: int = 0) -> dict:
    key = jax.random.key(seed)
    ks = iter(jax.random.split(key, 64))
    L = DEPTH
    nd = (DEPTH + 1) // 2
    nm = DEPTH // 2
    f32 = jnp.float32
    def normal(shape, scale):
        return jax.random.normal(next(ks), shape, f32) * scale
    def uniform(shape, lo, hi):
        return jax.random.uniform(next(ks), shape, f32, lo, hi)
    beta = DEEPNORM_BETA
    x = normal((BATCH, SEQ, D_MODEL), 1.0)
    offset = jax.random.randint(next(ks), (BATCH, 1), 0, MAX_POS_OFFSET, dtype=jnp.int32)
    positions = (offset + jnp.arange(SEQ, dtype=jnp.int32)[None, :]).astype(jnp.int32)
    inp = {
        'x': x,
        'positions': positions,
        'w_in': normal((L, D_MODEL, IN_WIDTH), D_MODEL ** -0.5),
        'mla_q_norm': 1.0 + normal((L, Q_LORA), 0.02),
        'mla_w_uq': normal((L, Q_LORA, MLA_HEADS * (MLA_NOPE + MLA_ROPE)), Q_LORA ** -0.5),
        'mla_kv_norm': 1.0 + normal((L, KV_LORA), 0.02),
        'mla_w_ukv': normal((L, KV_LORA, MLA_HEADS * (MLA_NOPE + MLA_V)), KV_LORA ** -0.5),
        'rwkv_mu': uniform((L, RWKV_IN), 0.0, 1.0),
        'rwkv_w0': uniform((L, RWKV_WIDTH), -6.0, -1.0),
        'rwkv_w_up': normal((L, DECAY_LORA, RWKV_WIDTH), 0.1 * DECAY_LORA ** -0.5),
        'rwkv_a0': normal((L, RWKV_WIDTH), 0.1),
        'rwkv_a_up': normal((L, ICLR_LORA, RWKV_WIDTH), ICLR_LORA ** -0.5),
        'rwkv_g_up': normal((L, GATE_LORA, RWKV_WIDTH), GATE_LORA ** -0.5),
        'rwkv_k_k': 0.85 + normal((L, RWKV_WIDTH), 0.02),
        'rwkv_k_a': 1.0 + normal((L, RWKV_WIDTH), 0.02),
        'rwkv_r_k': normal((L, RWKV_HEADS, RWKV_HEAD), 0.1),
        'rwkv_ln_g': 1.0 + normal((L, RWKV_WIDTH), 0.02),
        'rwkv_ln_b': normal((L, RWKV_WIDTH), 0.02),
        'rwkv_vres_down': normal((L - 1, RWKV_WIDTH, VRES_LORA), RWKV_WIDTH ** -0.5),
        'rwkv_vres_up': normal((L - 1, VRES_LORA, RWKV_WIDTH), VRES_LORA ** -0.5),
        'rwkv_vres_b': normal((L - 1, RWKV_WIDTH), 0.1),
        's5_a_re': -0.5 + normal((L, S5_GROUPS, S5_STATE), 0.01),
        's5_a_im': math.pi * jnp.arange(S5_STATE, dtype=f32) + normal((L, S5_GROUPS, S5_STATE), 0.01),
        's5_log_dt': uniform((L, S5_GROUPS), math.log(0.001), math.log(0.1)),
        's5_b_re': normal((L, S5_GROUPS, S5_STATE, S5_GROUP), (2 * S5_GROUP) ** -0.5),
        's5_b_im': normal((L, S5_GROUPS, S5_STATE, S5_GROUP), (2 * S5_GROUP) ** -0.5),
        's5_c_re': normal((L, S5_GROUPS, S5_GROUP, S5_STATE), S5_STATE ** -0.5),
        's5_c_im': normal((L, S5_GROUPS, S5_GROUP, S5_STATE), S5_STATE ** -0.5),
        's5_d': normal((L, S5_GROUPS, S5_GROUP), 1.0),
        's5_w_glu': normal((L, S5_WIDTH, S5_WIDTH), S5_WIDTH ** -0.5),
        's5_b_glu': normal((L, S5_WIDTH), 0.02),
        'w_branch': normal((L, N_BRANCH, BRANCH_WIDTH, D_MODEL), beta * BRANCH_WIDTH ** -0.5),
        'w_out': normal((L, D_MODEL, D_MODEL), beta * D_MODEL ** -0.5),
        'ln1_g': 1.0 + normal((L, D_MODEL), 0.02),
        'ln1_b': normal((L, D_MODEL), 0.02),
        'ln2_g': 1.0 + normal((L, D_MODEL), 0.02),
        'ln2_b': normal((L, D_MODEL), 0.02),
        'ffn_w_gate': normal((nd, D_MODEL, FFN_DIM), D_MODEL ** -0.5),
        'ffn_w_up': normal((nd, D_MODEL, FFN_DIM), D_MODEL ** -0.5),
        'ffn_w_down': normal((nd, FFN_DIM, D_MODEL), beta * FFN_DIM ** -0.5),
        'moe_router': normal((nm, D_MODEL, N_EXPERTS), D_MODEL ** -0.5),
        'moe_w_gate': normal((nm, N_EXPERTS, D_MODEL, EXPERT_DIM), D_MODEL ** -0.5),
        'moe_w_up': normal((nm, N_EXPERTS, D_MODEL, EXPERT_DIM), D_MODEL ** -0.5),
        'moe_w_down': normal((nm, N_EXPERTS, EXPERT_DIM, D_MODEL), beta * EXPERT_DIM ** -0.5),
    }
    return inp


def reference(x, positions, w_in, mla_q_norm, mla_w_uq, mla_kv_norm, mla_w_ukv,
              rwkv_mu, rwkv_w0, rwkv_w_up, rwkv_a0, rwkv_a_up, rwkv_g_up, rwkv_k_k, rwkv_k_a,
              rwkv_r_k, rwkv_ln_g, rwkv_ln_b, rwkv_vres_down, rwkv_vres_up, rwkv_vres_b,
              s5_a_re, s5_a_im, s5_log_dt, s5_b_re, s5_b_im, s5_c_re, s5_c_im, s5_d, s5_w_glu, s5_b_glu,
              w_branch, w_out, ln1_g, ln1_b, ln2_g, ln2_b,
              ffn_w_gate, ffn_w_up, ffn_w_down, moe_router, moe_w_gate, moe_w_up, moe_w_down):
    bsz, seq, _ = x.shape
    inv_freq = ROPE_THETA ** (-jnp.arange(0, MLA_ROPE, 2, dtype=jnp.float32) / MLA_ROPE)
    ang = positions.astype(jnp.float32)[..., None] * inv_freq
    cos, sin = jnp.cos(ang), jnp.sin(ang)
    o1 = MLA_IN
    o2 = o1 + RWKV_IN
    o3 = o2 + S5_WIDTH
    v_first = None
    for l in range(DEPTH):
        proj = x @ w_in[l]
        y_mla = _mla_branch(proj[..., :o1], cos, sin, mla_q_norm[l], mla_w_uq[l], mla_kv_norm[l], mla_w_ukv[l])
        vres = None if l == 0 else (rwkv_vres_down[l - 1], rwkv_vres_up[l - 1], rwkv_vres_b[l - 1])
        y_rwkv, v_first = _rwkv7_branch(proj[..., o1:o2], v_first, rwkv_mu[l], rwkv_w0[l], rwkv_w_up[l],
                                        rwkv_a0[l], rwkv_a_up[l], rwkv_g_up[l], rwkv_k_k[l], rwkv_k_a[l],
                                        rwkv_r_k[l], rwkv_ln_g[l], rwkv_ln_b[l], vres)
        y_s5 = _s5_branch(proj[..., o2:o3], s5_a_re[l], s5_a_im[l], s5_log_dt[l], s5_b_re[l], s5_b_im[l],
                          s5_c_re[l], s5_c_im[l], s5_d[l], s5_w_glu[l], s5_b_glu[l])
        gates = jax.nn.sigmoid(proj[..., o3:].reshape(bsz, seq, N_BRANCH, D_MODEL))
        branches = jnp.stack([y_mla, y_rwkv, y_s5], axis=2)
        mixed = jnp.sum(gates * jnp.einsum('bsnc,ncd->bsnd', branches, w_branch[l]), axis=2)
        x = _layer_norm(DEEPNORM_ALPHA * x + mixed @ w_out[l], ln1_g[l], ln1_b[l])
        if l % 2 == 0:
            f = _swiglu(x, ffn_w_gate[l // 2], ffn_w_up[l // 2], ffn_w_down[l // 2])
        else:
            f = _moe_swiglu(x, moe_router[l // 2], moe_w_gate[l // 2], moe_w_up[l // 2], moe_w_down[l // 2])
        x = _layer_norm(DEEPNORM_ALPHA * x + f, ln2_g[l], ln2_b[l])
    return x
```

```python
import functools
import math

import jax
import jax.numpy as jnp
from jax import lax
from jax.experimental import pallas as pl
from jax.experimental.pallas import tpu as pltpu

F32 = jnp.float32
BF16 = jnp.bfloat16

D_MODEL = 1024
MLA_HEADS = 8
MLA_NOPE = 64
MLA_ROPE = 32
MLA_V = 64
Q_LORA = 384
KV_LORA = 256
ROPE_THETA = 10000.0
NEG_INF = -1e30
RWKV_HEADS = 8
RWKV_HEAD = 64
RWKV_WIDTH = 512
DECAY_LORA = 64
ICLR_LORA = 64
GATE_LORA = 128
RWKV_GN_EPS = 64e-5
RWKV_IN = 3 * RWKV_WIDTH + DECAY_LORA + ICLR_LORA + GATE_LORA
S5_GROUP = 16
S5_GROUPS = 32
S5_WIDTH = 512
S5_STATE = 64
S5_LANES = S5_GROUPS * S5_STATE
N_BRANCH = 3
FFN_DIM = 2816
N_EXPERTS = 8
TOP_K = 2
EXPERT_DIM = 3584
MOE_BLOCK = 512
DEPTH = 4
DEEPNORM_ALPHA = (2 * DEPTH) ** 0.25

LANE = 128
MLA_PAD = 896
CHUNK = 64
VMEM_LIMIT = 56 * 1024 * 1024

NN = ((1,), (0,))
NT = ((1,), (1,))
TN = ((0,), (0,))


def _dg(a, b, dims=NN):
    return lax.dot_general(a, b, (dims, ((), ())), preferred_element_type=F32)


def _dot(a, b, dims=NN):
    return _dg(a.astype(BF16), b.astype(BF16), dims)


def _split2(a):
    hi = a.astype(BF16)
    lo = (a - hi.astype(F32)).astype(BF16)
    return hi, lo


def _split3(a):
    hi = a.astype(BF16)
    r1 = a - hi.astype(F32)
    mid = r1.astype(BF16)
    lo = (r1 - mid.astype(F32)).astype(BF16)
    return hi, mid, lo


def _dot3(a, b, dims=NN):
    ah, al = _split2(a)
    bh, bl = _split2(b)
    return _dg(ah, bh, dims) + (_dg(ah, bl, dims) + _dg(al, bh, dims))


def _dot_exact_lhs(c_bf16, x):
    hi, mid, lo = _split3(x)
    return _dg(c_bf16, hi) + (_dg(c_bf16, mid) + _dg(c_bf16, lo))


def _dot_exact_rhs(x, c_bf16):
    hi, mid, lo = _split3(x)
    return _dg(hi, c_bf16) + (_dg(mid, c_bf16) + _dg(lo, c_bf16))


def _layer_norm(z, g, b, eps=1e-5):
    zc = z - jnp.mean(z, axis=-1, keepdims=True)
    var = jnp.mean(zc * zc, axis=-1, keepdims=True)
    return zc * lax.rsqrt(var + eps) * g + b


def _rms(x, g, eps=1e-6):
    return x * lax.rsqrt(jnp.mean(x * x, axis=-1, keepdims=True) + eps) * g


def _sigmoid(x):
    return 1.0 / (1.0 + jnp.exp(-x))


def _silu(x):
    return x * _sigmoid(x)


def _softplus(z):
    return jnp.maximum(z, 0.0) + jnp.log(1.0 + jnp.exp(-jnp.abs(z)))


def _gelu_tanh(x):
    c = math.sqrt(2.0 / math.pi)
    return x * (0.5 * (1.0 + jnp.tanh(c * (x + 0.044715 * (x * x * x)))))


def _params(sem, vmem=VMEM_LIMIT):
    return pltpu.CompilerParams(dimension_semantics=sem, vmem_limit_bytes=vmem)


def _full(shape):
    nd = len(shape)
    return pl.BlockSpec(shape, lambda *_: (0,) * nd, pipeline_mode=pl.Buffered(1))


def _rope_kernel(pos_ref, freq_ref, c_ref, s_ref):
    ang = pos_ref[...].astype(F32) * freq_ref[...]
    lane = lax.broadcasted_iota(jnp.int32, ang.shape, 1)
    c_ref[...] = jnp.where(lane < MLA_NOPE + MLA_ROPE, jnp.cos(ang), 0.0)
    s_ref[...] = jnp.sin(ang)


def _rope_tables(positions, tm):
    n = positions.size
    half = MLA_ROPE // 2
    inv_freq = ROPE_THETA ** (-jnp.arange(0, MLA_ROPE, 2, dtype=F32) / MLA_ROPE)
    freq = jnp.concatenate([jnp.zeros((MLA_NOPE,), F32), inv_freq, inv_freq,
                            jnp.zeros((LANE - MLA_NOPE - 2 * half,), F32)])[None, :]
    out = jax.ShapeDtypeStruct((n, LANE), F32)
    return pl.pallas_call(
        _rope_kernel,
        grid=(n // tm,),
        in_specs=[pl.BlockSpec((tm, 1), lambda i: (i, 0)), _full((1, LANE))],
        out_specs=[pl.BlockSpec((tm, LANE), lambda i: (i, 0))] * 2,
        out_shape=[out, out],
        compiler_params=_params(("parallel",)),
        name="rope_tables",
    )(positions.reshape(n, 1), freq)


IN_SPLITS = (MLA_PAD, RWKV_IN, S5_WIDTH, N_BRANCH * D_MODEL)
IN_PAD = sum(IN_SPLITS)


def _prep_w_in(w):
    cq = w[..., :Q_LORA]
    ckv = w[..., Q_LORA:Q_LORA + KV_LORA]
    kpe = w[..., Q_LORA + KV_LORA:Q_LORA + KV_LORA + MLA_ROPE]
    half = MLA_ROPE // 2
    k1, k2 = kpe[..., :half], kpe[..., half:]
    z64 = jnp.zeros(w.shape[:-1] + (MLA_NOPE,), w.dtype)
    z32 = jnp.zeros(w.shape[:-1] + (LANE - MLA_NOPE - MLA_ROPE,), w.dtype)
    rest = w[..., Q_LORA + KV_LORA + MLA_ROPE:]
    return jnp.concatenate([cq, ckv, z64, k1, k2, z32, z64, -k2, k1, z32, rest], axis=-1).astype(BF16)


def _in_proj_kernel(x_ref, w_ref, mla_ref, rwkv_ref, s5_ref, gate_ref):
    x = x_ref[...].astype(BF16)
    o0 = 0
    o1 = o0 + MLA_PAD
    o2 = o1 + RWKV_IN
    o3 = o2 + S5_WIDTH
    mla_ref[...] = _dg(x, w_ref[:, o0:o1])
    rwkv_ref[...] = _dg(x, w_ref[:, o1:o2])
    s5_ref[...] = _dg(x, w_ref[:, o2:o3])
    for n in range(N_BRANCH):
        c0 = o3 + n * D_MODEL
        gate_ref[:, n * D_MODEL:(n + 1) * D_MODEL] = _sigmoid(_dg(x, w_ref[:, c0:c0 + D_MODEL]))


def _in_proj(x2, w, bsz, seq, tm):
    n = bsz * seq
    nt = seq // tm
    row = lambda b, i: (b * nt + i, 0)
    return pl.pallas_call(
        _in_proj_kernel,
        grid=(bsz, nt),
        in_specs=[pl.BlockSpec((tm, D_MODEL), row), _full((D_MODEL, IN_PAD))],
        out_specs=[pl.BlockSpec((tm, MLA_PAD), row),
                   pl.BlockSpec((tm, RWKV_IN), row),
                   pl.BlockSpec((tm, S5_WIDTH), lambda b, i: (i, b)),
                   pl.BlockSpec((tm, N_BRANCH * D_MODEL), row)],
        out_shape=[jax.ShapeDtypeStruct((n, MLA_PAD), F32),
                   jax.ShapeDtypeStruct((n, RWKV_IN), F32),
                   jax.ShapeDtypeStruct((seq, bsz * S5_WIDTH), F32),
                   jax.ShapeDtypeStruct((n, N_BRANCH * D_MODEL), F32)],
        compiler_params=_params(("parallel", "parallel")),
        name="in_proj",
    )(x2, w)


def _prep_mla_weights(w_uq, w_ukv):
    half = MLA_ROPE // 2
    wq = w_uq.reshape(Q_LORA, MLA_HEADS, MLA_NOPE + MLA_ROPE)
    nope, pe1, pe2 = wq[..., :MLA_NOPE], wq[..., MLA_NOPE:MLA_NOPE + half], wq[..., MLA_NOPE + half:]
    z32 = jnp.zeros((Q_LORA, MLA_HEADS, LANE - MLA_NOPE - MLA_ROPE), w_uq.dtype)
    z64 = jnp.zeros((Q_LORA, MLA_HEADS, MLA_NOPE), w_uq.dtype)
    wq_main = jnp.concatenate([nope, pe1, pe2, z32], -1).reshape(Q_LORA, MLA_HEADS * LANE)
    wq_rot = jnp.concatenate([z64, -pe2, pe1, z32], -1).reshape(Q_LORA, MLA_HEADS * LANE)
    wkv = w_ukv.reshape(KV_LORA, MLA_HEADS, MLA_NOPE + MLA_V)
    k_nope, v = wkv[..., :MLA_NOPE], wkv[..., MLA_NOPE:]
    zk = jnp.zeros((KV_LORA, MLA_HEADS, LANE - MLA_NOPE), w_ukv.dtype)
    wk = jnp.concatenate([k_nope, zk], -1).reshape(KV_LORA, MLA_HEADS * LANE)
    zv = jnp.zeros_like(v)
    even = (jnp.arange(MLA_HEADS) % 2 == 0)[None, :, None]
    wv = jnp.concatenate([jnp.where(even, v, zv), jnp.where(even, zv, v)], -1).reshape(KV_LORA, MLA_HEADS * LANE)
    return wq_main.astype(BF16), wq_rot.astype(BF16), wk.astype(BF16), wv.astype(BF16)


def _mla_prep_kernel(p_ref, c_ref, s_ref, qn_ref, kvn_ref, wq_ref, wqr_ref, wk_ref, wv_ref,
                     q_out, k_out, v_out):
    cos = c_ref[...]
    sin = s_ref[...]
    qn = _rms(p_ref[:, :Q_LORA], qn_ref[...]).astype(BF16)
    kvn = _rms(p_ref[:, Q_LORA:Q_LORA + KV_LORA], kvn_ref[...]).astype(BF16)
    o = Q_LORA + KV_LORA
    k_rope = p_ref[:, o:o + LANE] * cos + p_ref[:, o + LANE:o + 2 * LANE] * sin
    scale = (MLA_NOPE + MLA_ROPE) ** -0.5
    for h in range(MLA_HEADS):
        cols = slice(h * LANE, (h + 1) * LANE)
        q = _dg(qn, wq_ref[:, cols]) * cos + _dg(qn, wqr_ref[:, cols]) * sin
        q_out[0, h] = (q * scale).astype(BF16)
        k_out[0, h] = (_dg(kvn, wk_ref[:, cols]) + k_rope).astype(BF16)
        v_out[0, h] = _dg(kvn, wv_ref[:, cols]).astype(BF16)


def _mla_prep(p_mla, cos, sin, q_norm, kv_norm, wq, wqr, wk, wv, bsz, seq, tm):
    nt = seq // tm
    row = lambda b, i: (b * nt + i, 0)
    hw = MLA_HEADS * LANE
    out = jax.ShapeDtypeStruct((bsz, MLA_HEADS, seq, LANE), BF16)
    ospec = pl.BlockSpec((1, MLA_HEADS, tm, LANE), lambda b, i: (b, 0, i, 0))
    return pl.pallas_call(
        _mla_prep_kernel,
        grid=(bsz, nt),
        in_specs=[pl.BlockSpec((tm, MLA_PAD), row),
                  pl.BlockSpec((tm, LANE), row), pl.BlockSpec((tm, LANE), row),
                  _full((1, Q_LORA)), _full((1, KV_LORA)),
                  _full((Q_LORA, hw)), _full((Q_LORA, hw)), _full((KV_LORA, hw)), _full((KV_LORA, hw))],
        out_specs=[ospec, ospec, ospec],
        out_shape=[out, out, out],
        compiler_params=_params(("parallel", "parallel")),
        name="mla_prep",
    )(p_mla, cos, sin, q_norm, kv_norm, wq, wqr, wk, wv)


def _flash_kernel(q_ref, k_ref, v_ref, o_ref, *, tq):
    i = pl.program_id(2)
    row = lax.broadcasted_iota(jnp.int32, (tq, tq), 0)
    col = lax.broadcasted_iota(jnp.int32, (tq, tq), 1)
    causal = row >= col
    out = None
    for hh in range(2):
        q = q_ref[0, hh]

        def scores(j):
            start = pl.multiple_of(j * tq, tq)
            kj = k_ref[0, hh, pl.ds(start, tq), :]
            vj = v_ref[0, hh, pl.ds(start, tq), :]
            return _dg(q, kj, NT), vj

        def update(carry, s, vj):
            m, l, acc = carry
            m_new = jnp.maximum(m, jnp.max(s, axis=-1, keepdims=True))
            p = jnp.exp(s - m_new)
            alpha = jnp.exp(m - m_new)
            l = alpha * l + jnp.sum(p, axis=-1, keepdims=True)
            acc = alpha * acc + _dg(p.astype(BF16), vj)
            return m_new, l, acc

        def body(j, carry):
            s, vj = scores(j)
            return update(carry, s, vj)

        init = (jnp.full((tq, 1), NEG_INF, F32), jnp.zeros((tq, 1), F32), jnp.zeros((tq, LANE), F32))
        carry = lax.fori_loop(0, i, body, init)
        s, vj = scores(i)
        m, l, acc = update(carry, jnp.where(causal, s, NEG_INF), vj)
        o = acc / l
        out = o if out is None else out + o
    o_ref[...] = out


def _flash(q, k, v, bsz, seq, tq):
    nt = seq // tq
    return pl.pallas_call(
        functools.partial(_flash_kernel, tq=tq),
        grid=(bsz, MLA_HEADS // 2, nt),
        in_specs=[pl.BlockSpec((1, 2, tq, LANE), lambda b, h, i: (b, h, i, 0)),
                  pl.BlockSpec((1, 2, seq, LANE), lambda b, h, i: (b, h, 0, 0)),
                  pl.BlockSpec((1, 2, seq, LANE), lambda b, h, i: (b, h, 0, 0))],
        out_specs=pl.BlockSpec((tq, LANE), lambda b, h, i: (b * nt + i, h)),
        out_shape=jax.ShapeDtypeStruct((bsz * seq, MLA_HEADS * MLA_V), F32),
        compiler_params=_params(("parallel", "parallel", "arbitrary")),
        name="mla_flash",
    )(q, k, v)


def _rwkv_prep_kernel(*refs, tm, has_vres):
    if has_vres:
        (p_ref, prev_ref, vf_ref, mu_ref, w0_ref, wup_ref, a0_ref, aup_ref, gup_ref, kk_ref, ka_ref,
         rk_ref, vd_ref, vu_ref, vb_ref, tri_ref, blk_ref, sel_ref, bd_ref,
         rh_o, ah_o, bh_o, kh_o, bt_o, kt_o, v_o, bon_o, g_o, gam_o) = refs
    else:
        (p_ref, prev_ref, mu_ref, w0_ref, wup_ref, a0_ref, aup_ref, gup_ref, kk_ref, ka_ref,
         rk_ref, tri_ref, blk_ref, sel_ref, bd_ref,
         rh_o, ah_o, bh_o, kh_o, bt_o, kt_o, v_o, bon_o, g_o, gam_o) = refs
    i = pl.program_id(1)
    p = p_ref[...]
    first = jnp.where(i == 0, 0.0, prev_ref[7:8, :])
    rows = lax.broadcasted_iota(jnp.int32, p.shape, 0)
    prev = jnp.where(rows == 0, first, pltpu.roll(p, 1, axis=0))
    ps = p + (prev - p) * mu_ref[...]
    W = RWKV_WIDTH
    r, k, v = ps[:, :W], ps[:, W:2 * W], ps[:, 2 * W:3 * W]
    lora = ps[:, 3 * W:3 * W + DECAY_LORA + ICLR_LORA]
    g_d = ps[:, 3 * W + DECAY_LORA + ICLR_LORA:]
    w_log = -_softplus(-(w0_ref[...] + _dot(jnp.tanh(lora), wup_ref[...]))) - 0.5
    lw = -jnp.exp(w_log)
    iclr = _sigmoid(a0_ref[...] + _dot(lora, aup_ref[...]))
    g_o[...] = _dot(_sigmoid(g_d), gup_ref[...])
    if has_vres:
        mix = _sigmoid(vb_ref[...] + _dot(_dot(v, vd_ref[...]), vu_ref[...]))
        v = v + (vf_ref[...] - v) * mix
    bd = bd_ref[...]
    kk = k * kk_ref[...]
    kk = kk / jnp.maximum(jnp.sqrt(_dot_exact_rhs(kk * kk, bd)), 1e-12)
    k_h = k * (1.0 + (iclr - 1.0) * ka_ref[...])
    bon_o[...] = _dot_exact_rhs(r * k_h * rk_ref[...], bd) * v
    v_o[...] = v
    cum = _dot_exact_lhs(tri_ref[...], lw)
    tot = _dot_exact_lhs(blk_ref[...], lw)
    b = kk * iclr
    e_neg = jnp.exp(-cum)
    e_rem = jnp.exp(tot - cum)
    rh_o[...] = r * jnp.exp(cum)
    ah_o[...] = -kk * jnp.exp(cum - lw)
    bh_o[...] = b * e_neg
    kh_o[...] = k_h * e_neg
    bt_o[...] = b * e_rem
    kt_o[...] = k_h * e_rem
    gam_o[...] = jnp.exp(_dot_exact_lhs(sel_ref[...], lw))


def _rwkv_prep(p_rwkv, v_first, wts, consts, bsz, seq, tm):
    n = bsz * seq
    nt = seq // tm
    has_vres = v_first is not None
    row = lambda b, i: (b * nt + i, 0)
    prev_map = lambda b, i: (jnp.maximum((b * nt + i) * (tm // 8) - 1, 0), 0)
    wide = pl.BlockSpec((tm, RWKV_WIDTH), row)
    ins = [p_rwkv, p_rwkv]
    specs = [pl.BlockSpec((tm, RWKV_IN), row), pl.BlockSpec((8, RWKV_IN), prev_map)]
    if has_vres:
        ins.append(v_first)
        specs.append(wide)
    names = ["mu", "w0", "wup", "a0", "aup", "gup", "kk", "ka", "rk"]
    if has_vres:
        names += ["vd", "vu", "vb"]
    for nm in names:
        ins.append(wts[nm])
        specs.append(_full(wts[nm].shape))
    for c in consts:
        ins.append(c)
        specs.append(_full(c.shape))
    o = jax.ShapeDtypeStruct((n, RWKV_WIDTH), F32)
    nch = tm // CHUNK
    return pl.pallas_call(
        functools.partial(_rwkv_prep_kernel, tm=tm, has_vres=has_vres),
        grid=(bsz, nt),
        in_specs=specs,
        out_specs=[wide] * 9 + [pl.BlockSpec((nch, RWKV_WIDTH), lambda b, i: (b * nt + i, 0))],
        out_shape=[o] * 9 + [jax.ShapeDtypeStruct((n // CHUNK, RWKV_WIDTH), F32)],
        compiler_params=_params(("parallel", "parallel")),
        name="rwkv_prep",
    )(*ins)


def _rwkv_scan_kernel(rh_ref, ah_ref, bh_ref, kh_ref, bt_ref, kt_ref, v_ref, bon_ref, g_ref, gam_ref,
                      lng_ref, lnb_ref, o_ref, s_ref):
    c = pl.program_id(1)

    @pl.when(c == 0)
    def _():
        s_ref[...] = jnp.zeros_like(s_ref)

    C = CHUNK
    rr = lax.broadcasted_iota(jnp.int32, (C, C), 0)
    cc = lax.broadcasted_iota(jnp.int32, (C, C), 1)
    strict = rr > cc
    incl = rr >= cc
    eye = jnp.where(rr == cc, 1.0, 0.0).astype(F32)
    gam = gam_ref[pl.ds(c % 8, 1), :]
    for h in range(RWKV_HEADS):
        sl = slice(h * RWKV_HEAD, (h + 1) * RWKV_HEAD)
        a_hat, r_hat = ah_ref[:, sl], rh_ref[:, sl]
        b_hat, k_hat = bh_ref[:, sl], kh_ref[:, sl]
        v = v_ref[:, sl]
        s0 = s_ref[h]
        ar = jnp.concatenate([a_hat, r_hat], axis=0)
        bk = jnp.concatenate([b_hat, k_hat], axis=0)
        gm = _dot3(ar, bk, NT)
        a_ab = jnp.where(strict, gm[:C, :C], 0.0)
        a_ak = jnp.where(strict, gm[:C, C:], 0.0)
        a_rb = jnp.where(incl, gm[C:, :C], 0.0)
        a_rk = jnp.where(incl, gm[C:, C:], 0.0)
        pw = a_ab
        tinv = eye + a_ab
        for _ in range(int(math.log2(C)) - 1):
            pw = _dot3(pw, pw)
            tinv = tinv + _dot3(pw, tinv)
        xs = _dot3(ar, s0, NT)
        u = _dot3(tinv, xs[:C] + _dot3(a_ak, v))
        o = xs[C:] + _dot3(a_rb, u) + _dot3(a_rk, v)
        s_ref[h] = s0 * gam[:, sl] + _dot3(u, bt_ref[:, sl], TN) + _dot3(v, kt_ref[:, sl], TN)
        oc = o - jnp.mean(o, axis=-1, keepdims=True)
        on = oc * lax.rsqrt(jnp.mean(oc * oc, axis=-1, keepdims=True) + RWKV_GN_EPS)
        y = on * lng_ref[:, sl] + lnb_ref[:, sl]
        o_ref[:, sl] = (y + bon_ref[:, sl]) * g_ref[:, sl]


def _rwkv_scan(prep, ln_g, ln_b, bsz, seq):
    n = bsz * seq
    nc = seq // CHUNK
    row = lambda b, c: (b * nc + c, 0)
    wide = pl.BlockSpec((CHUNK, RWKV_WIDTH), row)
    return pl.pallas_call(
        _rwkv_scan_kernel,
        grid=(bsz, nc),
        in_specs=[wide] * 9 + [pl.BlockSpec((8, RWKV_WIDTH), lambda b, c: ((b * nc + c) // 8, 0)),
                               _full((1, RWKV_WIDTH)), _full((1, RWKV_WIDTH))],
        out_specs=wide,
        out_shape=jax.ShapeDtypeStruct((n, RWKV_WIDTH), F32),
        scratch_shapes=[pltpu.VMEM((RWKV_HEADS, RWKV_HEAD, RWKV_HEAD), F32)],
        compiler_params=_params(("parallel", "arbitrary")),
        name="rwkv_scan",
    )(*prep, ln_g, ln_b)


def _rwkv_consts(tm):
    t = jnp.arange(tm)
    same = (t[:, None] // CHUNK) == (t[None, :] // CHUNK)
    tri = (same & (t[:, None] >= t[None, :])).astype(BF16)
    blk = same.astype(BF16)
    sel = ((jnp.arange(tm // CHUNK)[:, None]) == (t[None, :] // CHUNK)).astype(BF16)
    ch = jnp.arange(RWKV_WIDTH) // RWKV_HEAD
    bd = (ch[:, None] == ch[None, :]).astype(BF16)
    return tri, blk, sel, bd


def _prep_s5(a_re, a_im, log_dt, b_re, b_im, c_re, c_im, d, bsz):
    dt = jnp.exp(log_dt.astype(F32))[:, None]
    lam_re = jnp.minimum(a_re.astype(F32), -1e-4)
    lam_im = a_im.astype(F32)
    mag = jnp.exp(dt * lam_re)
    ab_re, ab_im = mag * jnp.cos(dt * lam_im), mag * jnp.sin(dt * lam_im)
    den = lam_re * lam_re + lam_im * lam_im
    f_re = ((ab_re - 1.0) * lam_re + ab_im * lam_im) / den
    f_im = (ab_im * lam_re - (ab_re - 1.0) * lam_im) / den
    bb_re = f_re[..., None] * b_re - f_im[..., None] * b_im
    bb_im = f_re[..., None] * b_im + f_im[..., None] * b_re
    eye = jnp.eye(S5_GROUPS, dtype=F32)

    def expand_b(bb):
        return jnp.einsum('gpc,gh->gchp', bb, eye).reshape(S5_WIDTH, S5_LANES)

    def expand_c(cm):
        return jnp.einsum('gcp,gh->gphc', cm.astype(F32), eye).reshape(S5_LANES, S5_WIDTH)

    tile = lambda a: jnp.broadcast_to(a.reshape(1, S5_LANES), (bsz, S5_LANES))
    return (expand_b(bb_re).astype(BF16), expand_b(bb_im).astype(BF16), tile(ab_re), tile(ab_im),
            expand_c(c_re).astype(BF16), expand_c(c_im).astype(BF16), d.astype(F32).reshape(1, S5_WIDTH))


def _s5_kernel(u_ref, bre_ref, bim_ref, are_ref, aim_ref, cre_ref, cim_ref, d_ref, wglu_ref, bglu_ref,
               o_ref, xre_s, xim_s, st_re, st_im, *, ts, bsz):
    @pl.when(pl.program_id(0) == 0)
    def _():
        st_re[...] = jnp.zeros_like(st_re)
        st_im[...] = jnp.zeros_like(st_im)

    u = u_ref[...]
    ub = u.astype(BF16)
    xre_s[...] = _dg(ub, bre_ref[...])
    xim_s[...] = _dg(ub, bim_ref[...])
    def step(t, carry):
        x_re, x_im = carry
        a_re = are_ref[...]
        a_im = aim_ref[...]
        r0 = pl.multiple_of(t * bsz, bsz)
        n_re = a_re * x_re - a_im * x_im + xre_s[pl.ds(r0, bsz), :]
        n_im = a_re * x_im + a_im * x_re + xim_s[pl.ds(r0, bsz), :]
        xre_s[pl.ds(r0, bsz), :] = n_re
        xim_s[pl.ds(r0, bsz), :] = n_im
        return n_re, n_im

    x_re, x_im = lax.fori_loop(0, ts, step, (st_re[...], st_im[...]))
    st_re[...] = x_re
    st_im[...] = x_im
    y = _dot(xre_s[...], cre_ref[...]) - _dot(xim_s[...], cim_ref[...]) + d_ref[...] * u
    y = _gelu_tanh(y)
    o_ref[...] = y * _sigmoid(_dot(y, wglu_ref[...]) + bglu_ref[...])


def _s5(u_tm, s5w, w_glu, b_glu, bsz, seq, ts):
    rows = ts * bsz
    bre, bim, are, aim, cre, cim, d = s5w
    tile = pl.BlockSpec((rows, S5_WIDTH), lambda i: (i, 0))
    return pl.pallas_call(
        functools.partial(_s5_kernel, ts=ts, bsz=bsz),
        grid=(seq // ts,),
        in_specs=[tile, _full(bre.shape), _full(bim.shape), _full(are.shape), _full(aim.shape),
                  _full(cre.shape), _full(cim.shape), _full(d.shape), _full(w_glu.shape), _full(b_glu.shape)],
        out_specs=tile,
        out_shape=jax.ShapeDtypeStruct((seq * bsz, S5_WIDTH), F32),
        scratch_shapes=[pltpu.VMEM((rows, S5_LANES), F32), pltpu.VMEM((rows, S5_LANES), F32),
                        pltpu.VMEM((bsz, S5_LANES), F32), pltpu.VMEM((bsz, S5_LANES), F32)],
        compiler_params=_params(("arbitrary",)),
        name="s5_scan",
    )(u_tm, bre, bim, are, aim, cre, cim, d, w_glu, b_glu)


def _merge_kernel(*refs, with_router):
    if with_router:
        x_ref, ym_ref, yr_ref, ys_ref, gt_ref, wb_ref, wo_ref, g_ref, b_ref, rt_ref, o_ref, lg_ref = refs
    else:
        x_ref, ym_ref, yr_ref, ys_ref, gt_ref, wb_ref, wo_ref, g_ref, b_ref, o_ref = refs
    mixed = None
    for n, y_ref in enumerate((ym_ref, yr_ref, ys_ref)):
        t = gt_ref[:, n * D_MODEL:(n + 1) * D_MODEL] * _dot(y_ref[...], wb_ref[n])
        mixed = t if mixed is None else mixed + t
    z = DEEPNORM_ALPHA * x_ref[...] + _dot(mixed, wo_ref[...])
    out = _layer_norm(z, g_ref[...], b_ref[...])
    o_ref[...] = out
    if with_router:
        lg_ref[...] = _dot3(out, rt_ref[...])


def _merge(x2, y_mla, y_rwkv, y_s5_tm, gates, wb, wo, g, b, router, bsz, seq, tm):
    n = bsz * seq
    nt = seq // tm
    row = lambda bb, i: (bb * nt + i, 0)
    with_router = router is not None
    ins = [x2, y_mla, y_rwkv, y_s5_tm, gates, wb, wo, g, b]
    specs = [pl.BlockSpec((tm, D_MODEL), row),
             pl.BlockSpec((tm, 512), row), pl.BlockSpec((tm, 512), row),
             pl.BlockSpec((tm, S5_WIDTH), lambda bb, i: (i, bb)),
             pl.BlockSpec((tm, N_BRANCH * D_MODEL), row),
             _full(wb.shape), _full(wo.shape), _full(g.shape), _full(b.shape)]
    out_specs = [pl.BlockSpec((tm, D_MODEL), row)]
    out_shape = [jax.ShapeDtypeStruct((n, D_MODEL), F32)]
    if with_router:
        ins.append(router)
        specs.append(_full(router.shape))
        out_specs.append(pl.BlockSpec((tm, LANE), row))
        out_shape.append(jax.ShapeDtypeStruct((n, LANE), F32))
    res = pl.pallas_call(
        functools.partial(_merge_kernel, with_router=with_router),
        grid=(bsz, nt),
        in_specs=specs,
        out_specs=out_specs,
        out_shape=out_shape,
        compiler_params=_params(("parallel", "parallel")),
        name="merge_ln1",
    )(*ins)
    return res if with_router else (res[0], None)


def _ffn_kernel(x_ref, wg_ref, wu_ref, wd_ref, g_ref, b_ref, o_ref, acc_ref):
    j = pl.program_id(1)
    x = x_ref[...]
    xb = x.astype(BF16)
    h = _silu(_dg(xb, wg_ref[...])) * _dg(xb, wu_ref[...])
    part = _dot(h, wd_ref[...])

    @pl.when(j == 0)
    def _():
        acc_ref[...] = part

    @pl.when(j > 0)
    def _():
        acc_ref[...] += part

    @pl.when(j == pl.num_programs(1) - 1)
    def _():
        o_ref[...] = _layer_norm(DEEPNORM_ALPHA * x + acc_ref[...], g_ref[...], b_ref[...])


def _ffn(x2, wg, wu, wd, g, b, tm, tf):
    n = x2.shape[0]
    return pl.pallas_call(
        _ffn_kernel,
        grid=(n // tm, FFN_DIM // tf),
        in_specs=[pl.BlockSpec((tm, D_MODEL), lambda i, j: (i, 0)),
                  pl.BlockSpec((D_MODEL, tf), lambda i, j: (0, j)),
                  pl.BlockSpec((D_MODEL, tf), lambda i, j: (0, j)),
                  pl.BlockSpec((tf, D_MODEL), lambda i, j: (j, 0)),
                  _full(g.shape), _full(b.shape)],
        out_specs=pl.BlockSpec((tm, D_MODEL), lambda i, j: (i, 0)),
        out_shape=jax.ShapeDtypeStruct((n, D_MODEL), F32),
        scratch_shapes=[pltpu.VMEM((tm, D_MODEL), F32)],
        compiler_params=_params(("parallel", "arbitrary")),
        name="ffn_ln2",
    )(x2, wg, wu, wd, g, b)


def _moe_kernel(be_ref, nu_ref, xs_ref, sw_ref, wg_ref, wu_ref, wd_ref, o_ref, acc_ref):
    i = pl.program_id(0)
    j = pl.program_id(1)

    @pl.when(i < nu_ref[0])
    def _():
        xb = xs_ref[...]
        h = _silu(_dg(xb, wg_ref[0])) * _dg(xb, wu_ref[0])
        part = _dot(h, wd_ref[0])

        @pl.when(j == 0)
        def _():
            acc_ref[...] = part

        @pl.when(j > 0)
        def _():
            acc_ref[...] += part

        @pl.when(j == pl.num_programs(1) - 1)
        def _():
            o_ref[...] = acc_ref[...] * sw_ref[...]


def _moe_experts(xs, slot_w, block_e, n_used, wg, wu, wd, tf):
    n_slots = xs.shape[0]
    n_blocks = n_slots // MOE_BLOCK
    nf = EXPERT_DIM // tf

    def blk(i, nu):
        return jnp.minimum(i, nu[0] - 1)

    def fcol(i, j, nu):
        return jnp.where(i < nu[0], j, nf - 1)

    grid_spec = pltpu.PrefetchScalarGridSpec(
        num_scalar_prefetch=2,
        grid=(n_blocks, nf),
        in_specs=[pl.BlockSpec((MOE_BLOCK, D_MODEL), lambda i, j, be, nu: (blk(i, nu), 0)),
                  pl.BlockSpec((MOE_BLOCK, 1), lambda i, j, be, nu: (blk(i, nu), 0)),
                  pl.BlockSpec((1, D_MODEL, tf), lambda i, j, be, nu: (be[blk(i, nu)], 0, fcol(i, j, nu))),
                  pl.BlockSpec((1, D_MODEL, tf), lambda i, j, be, nu: (be[blk(i, nu)], 0, fcol(i, j, nu))),
                  pl.BlockSpec((1, tf, D_MODEL), lambda i, j, be, nu: (be[blk(i, nu)], fcol(i, j, nu), 0))],
        out_specs=pl.BlockSpec((MOE_BLOCK, D_MODEL), lambda i, j, be, nu: (blk(i, nu), 0)),
        scratch_shapes=[pltpu.VMEM((MOE_BLOCK, D_MODEL), F32)],
    )
    return pl.pallas_call(
        _moe_kernel,
        grid_spec=grid_spec,
        out_shape=jax.ShapeDtypeStruct((n_slots, D_MODEL), F32),
        compiler_params=_params(("arbitrary", "arbitrary")),
        name="moe_experts",
    )(block_e, n_used, xs, slot_w, wg, wu, wd)


def _combine_kernel(x_ref, y0_ref, y1_ref, g_ref, b_ref, o_ref):
    z = DEEPNORM_ALPHA * x_ref[...] + (y0_ref[...] + y1_ref[...])
    o_ref[...] = _layer_norm(z, g_ref[...], b_ref[...])


def _combine(x2, y0, y1, g, b, tm):
    n = x2.shape[0]
    tile = pl.BlockSpec((tm, D_MODEL), lambda i: (i, 0))
    return pl.pallas_call(
        _combine_kernel,
        grid=(n // tm,),
        in_specs=[tile, tile, tile, _full(g.shape), _full(b.shape)],
        out_specs=tile,
        out_shape=jax.ShapeDtypeStruct((n, D_MODEL), F32),
        compiler_params=_params(("parallel",)),
        name="moe_combine_ln2",
    )(x2, y0, y1, g, b)


def _moe(h2, logits, wg, wu, wd, g, b, tm, tf):
    n_tok = h2.shape[0]
    n_pair = n_tok * TOP_K
    top_logit, top_e = lax.top_k(logits[:, :N_EXPERTS], TOP_K)
    top_w = jax.nn.softmax(top_logit, axis=-1)
    pair_e = top_e.reshape(-1).astype(jnp.int32)
    order = jnp.argsort(pair_e)
    se = pair_e[order]
    counts = jnp.bincount(pair_e, length=N_EXPERTS).astype(jnp.int32)
    start = jnp.cumsum(counts) - counts
    padded = (counts + MOE_BLOCK - 1) // MOE_BLOCK * MOE_BLOCK
    pad_end = jnp.cumsum(padded)
    pad_start = pad_end - padded
    dest = pad_start[se] + jnp.arange(n_pair, dtype=jnp.int32) - start[se]
    n_blocks = n_pair // MOE_BLOCK + N_EXPERTS
    n_slots = n_blocks * MOE_BLOCK
    slot_tok = jnp.zeros((n_slots,), jnp.int32).at[dest].set((order // TOP_K).astype(jnp.int32))
    slot_w = jnp.zeros((n_slots,), F32).at[dest].set(top_w.reshape(-1)[order])
    block_e = jnp.minimum(jnp.searchsorted(pad_end, jnp.arange(n_blocks, dtype=jnp.int32) * MOE_BLOCK,
                                           side='right'), N_EXPERTS - 1).astype(jnp.int32)
    n_used = (pad_end[-1] // MOE_BLOCK).astype(jnp.int32).reshape(1)
    pair_slot = jnp.zeros((n_pair,), jnp.int32).at[order].set(dest)
    xs = jnp.take(h2.astype(BF16), slot_tok, axis=0)
    y = _moe_experts(xs, slot_w[:, None], block_e, n_used, wg, wu, wd, tf)
    ps = pair_slot.reshape(n_tok, TOP_K)
    y0 = jnp.take(y, ps[:, 0], axis=0)
    y1 = jnp.take(y, ps[:, 1], axis=0)
    return _combine(h2, y0, y1, g, b, tm)


def kernel(x, positions, w_in, mla_q_norm, mla_w_uq, mla_kv_norm, mla_w_ukv,
           rwkv_mu, rwkv_w0, rwkv_w_up, rwkv_a0, rwkv_a_up, rwkv_g_up, rwkv_k_k, rwkv_k_a,
           rwkv_r_k, rwkv_ln_g, rwkv_ln_b, rwkv_vres_down, rwkv_vres_up, rwkv_vres_b,
           s5_a_re, s5_a_im, s5_log_dt, s5_b_re, s5_b_im, s5_c_re, s5_c_im, s5_d, s5_w_glu, s5_b_glu,
           w_branch, w_out, ln1_g, ln1_b, ln2_g, ln2_b,
           ffn_w_gate, ffn_w_up, ffn_w_down, moe_router, moe_w_gate, moe_w_up, moe_w_down):
    bsz, seq, _ = x.shape
    n = bsz * seq
    depth = w_in.shape[0]
    tm = min(512, seq)
    tq = min(512, seq)
    ts = min(64, seq)
    row = lambda a: a.reshape(1, -1)

    cos, sin = _rope_tables(positions, tm)
    w_in_p = _prep_w_in(w_in)
    consts = _rwkv_consts(tm)
    zpad = jnp.zeros((DECAY_LORA, RWKV_WIDTH), F32)

    h = x.reshape(n, D_MODEL)
    v_first = None
    for l in range(depth):
        p_mla, p_rwkv, u_s5, gates = _in_proj(h, w_in_p[l], bsz, seq, tm)

        wq, wqr, wk, wv = _prep_mla_weights(mla_w_uq[l], mla_w_ukv[l])
        q, k, v = _mla_prep(p_mla, cos, sin, row(mla_q_norm[l]), row(mla_kv_norm[l]),
                            wq, wqr, wk, wv, bsz, seq, tm)
        y_mla = _flash(q, k, v, bsz, seq, tq)

        wts = {
            "mu": row(rwkv_mu[l]), "w0": row(rwkv_w0[l]),
            "wup": jnp.concatenate([rwkv_w_up[l], zpad], 0).astype(BF16),
            "a0": row(rwkv_a0[l]),
            "aup": jnp.concatenate([zpad, rwkv_a_up[l]], 0).astype(BF16),
            "gup": rwkv_g_up[l].astype(BF16),
            "kk": row(rwkv_k_k[l]), "ka": row(rwkv_k_a[l]), "rk": row(rwkv_r_k[l]),
        }
        if l > 0:
            wts["vd"] = rwkv_vres_down[l - 1].astype(BF16)
            wts["vu"] = rwkv_vres_up[l - 1].astype(BF16)
            wts["vb"] = row(rwkv_vres_b[l - 1])
        prep = _rwkv_prep(p_rwkv, v_first, wts, consts, bsz, seq, tm)
        if l == 0:
            v_first = prep[6]
        y_rwkv = _rwkv_scan(prep, row(rwkv_ln_g[l]), row(rwkv_ln_b[l]), bsz, seq)

        s5w = _prep_s5(s5_a_re[l], s5_a_im[l], s5_log_dt[l], s5_b_re[l], s5_b_im[l],
                       s5_c_re[l], s5_c_im[l], s5_d[l], bsz)
        y_s5 = _s5(u_s5.reshape(seq * bsz, S5_WIDTH), s5w, s5_w_glu[l].astype(BF16), row(s5_b_glu[l]),
                   bsz, seq, ts)

        moe_layer = l % 2 == 1
        router = None
        if moe_layer:
            router = jnp.pad(moe_router[l // 2], ((0, 0), (0, LANE - N_EXPERTS)))
        h, logits = _merge(h, y_mla, y_rwkv, y_s5.reshape(seq, bsz * S5_WIDTH), gates,
                           w_branch[l].astype(BF16), w_out[l].astype(BF16), row(ln1_g[l]), row(ln1_b[l]),
                           router, bsz, seq, tm)
        if moe_layer:
            h = _moe(h, logits, moe_w_gate[l // 2].astype(BF16), moe_w_up[l // 2].astype(BF16),
                     moe_w_down[l // 2].astype(BF16), row(ln2_g[l]), row(ln2_b[l]), tm, tf=896)
        else:
            h = _ffn(h, ffn_w_gate[l // 2].astype(BF16), ffn_w_up[l // 2].astype(BF16),
                     ffn_w_down[l // 2].astype(BF16), row(ln2_g[l]), row(ln2_b[l]), tm, tf=1408)
    return h.reshape(bsz, seq, D_MODEL)
```

```python
import functools
import math

import jax
import jax.numpy as jnp
from jax import lax
from jax.experimental import pallas as pl
from jax.experimental.pallas import tpu as pltpu

F32 = jnp.float32
BF16 = jnp.bfloat16

D_MODEL = 1024
MLA_HEADS = 8
MLA_NOPE = 64
MLA_ROPE = 32
MLA_V = 64
Q_LORA = 384
KV_LORA = 256
ROPE_THETA = 10000.0
NEG_INF = -1e30
RWKV_HEADS = 8
RWKV_HEAD = 64
RWKV_WIDTH = 512
DECAY_LORA = 64
ICLR_LORA = 64
GATE_LORA = 128
RWKV_GN_EPS = 64e-5
RWKV_IN = 3 * RWKV_WIDTH + DECAY_LORA + ICLR_LORA + GATE_LORA
S5_GROUP = 16
S5_GROUPS = 32
S5_WIDTH = 512
S5_STATE = 64
S5_LANES = S5_GROUPS * S5_STATE
N_BRANCH = 3
FFN_DIM = 2816
N_EXPERTS = 8
TOP_K = 2
EXPERT_DIM = 3584
MOE_BLOCK = 512
DEPTH = 4
DEEPNORM_ALPHA = (2 * DEPTH) ** 0.25

LANE = 128
MLA_PAD = 896
CHUNK = 64
RWKV_NCH = 4
VMEM_LIMIT = 56 * 1024 * 1024

NN = ((1,), (0,))
NT = ((1,), (1,))
TN = ((0,), (0,))


def _dg(a, b, dims=NN):
    return lax.dot_general(a, b, (dims, ((), ())), preferred_element_type=F32)


def _dot(a, b, dims=NN):
    return _dg(a.astype(BF16), b.astype(BF16), dims)


def _split2(a):
    hi = a.astype(BF16)
    lo = (a - hi.astype(F32)).astype(BF16)
    return hi, lo


def _split3(a):
    hi = a.astype(BF16)
    r1 = a - hi.astype(F32)
    mid = r1.astype(BF16)
    lo = (r1 - mid.astype(F32)).astype(BF16)
    return hi, mid, lo


def _dot3(a, b, dims=NN):
    ah, al = _split2(a)
    bh, bl = _split2(b)
    return _dg(ah, bh, dims) + (_dg(ah, bl, dims) + _dg(al, bh, dims))


def _dot_exact_lhs(c_bf16, x):
    hi, mid, lo = _split3(x)
    return _dg(c_bf16, hi) + (_dg(c_bf16, mid) + _dg(c_bf16, lo))


def _dot_exact_rhs(x, c_bf16):
    hi, mid, lo = _split3(x)
    return _dg(hi, c_bf16) + (_dg(mid, c_bf16) + _dg(lo, c_bf16))


def _layer_norm(z, g, b, eps=1e-5):
    zc = z - jnp.mean(z, axis=-1, keepdims=True)
    var = jnp.mean(zc * zc, axis=-1, keepdims=True)
    return zc * lax.rsqrt(var + eps) * g + b


def _rms(x, g, eps=1e-6):
    return x * lax.rsqrt(jnp.mean(x * x, axis=-1, keepdims=True) + eps) * g


def _sigmoid(x):
    return 1.0 / (1.0 + jnp.exp(-x))


def _silu(x):
    return x * _sigmoid(x)


def _softplus(z):
    return jnp.maximum(z, 0.0) + jnp.log(1.0 + jnp.exp(-jnp.abs(z)))


def _gelu_tanh(x):
    c = math.sqrt(2.0 / math.pi)
    return x * (0.5 * (1.0 + jnp.tanh(c * (x + 0.044715 * (x * x * x)))))


def _params(sem, vmem=VMEM_LIMIT):
    return pltpu.CompilerParams(dimension_semantics=sem, vmem_limit_bytes=vmem)


def _full(shape):
    nd = len(shape)
    return pl.BlockSpec(shape, lambda *_: (0,) * nd, pipeline_mode=pl.Buffered(1))


def _rope_kernel(pos_ref, freq_ref, c_ref, s_ref):
    ang = pos_ref[...].astype(F32) * freq_ref[...]
    lane = lax.broadcasted_iota(jnp.int32, ang.shape, 1)
    c_ref[...] = jnp.where(lane < MLA_NOPE + MLA_ROPE, jnp.cos(ang), 0.0)
    s_ref[...] = jnp.sin(ang)


def _rope_tables(positions, tm):
    n = positions.size
    half = MLA_ROPE // 2
    inv_freq = ROPE_THETA ** (-jnp.arange(0, MLA_ROPE, 2, dtype=F32) / MLA_ROPE)
    freq = jnp.concatenate([jnp.zeros((MLA_NOPE,), F32), inv_freq, inv_freq,
                            jnp.zeros((LANE - MLA_NOPE - 2 * half,), F32)])[None, :]
    out = jax.ShapeDtypeStruct((n, LANE), F32)
    return pl.pallas_call(
        _rope_kernel,
        grid=(n // tm,),
        in_specs=[pl.BlockSpec((tm, 1), lambda i: (i, 0)), _full((1, LANE))],
        out_specs=[pl.BlockSpec((tm, LANE), lambda i: (i, 0))] * 2,
        out_shape=[out, out],
        compiler_params=_params(("parallel",)),
        name="rope_tables",
    )(positions.reshape(n, 1), freq)


IN_SPLITS = (MLA_PAD, RWKV_IN, S5_WIDTH, N_BRANCH * D_MODEL)
IN_PAD = sum(IN_SPLITS)


def _prep_w_in(w):
    cq = w[..., :Q_LORA]
    ckv = w[..., Q_LORA:Q_LORA + KV_LORA]
    kpe = w[..., Q_LORA + KV_LORA:Q_LORA + KV_LORA + MLA_ROPE]
    half = MLA_ROPE // 2
    k1, k2 = kpe[..., :half], kpe[..., half:]
    z64 = jnp.zeros(w.shape[:-1] + (MLA_NOPE,), w.dtype)
    z32 = jnp.zeros(w.shape[:-1] + (LANE - MLA_NOPE - MLA_ROPE,), w.dtype)
    rest = w[..., Q_LORA + KV_LORA + MLA_ROPE:]
    return jnp.concatenate([cq, ckv, z64, k1, k2, z32, z64, -k2, k1, z32, rest], axis=-1).astype(BF16)


def _in_proj_kernel(x_ref, w_ref, mla_ref, rwkv_ref, s5_ref, gate_ref):
    x = x_ref[...].astype(BF16)
    o0 = 0
    o1 = o0 + MLA_PAD
    o2 = o1 + RWKV_IN
    o3 = o2 + S5_WIDTH
    mla_ref[...] = _dg(x, w_ref[:, o0:o1])
    rwkv_ref[...] = _dg(x, w_ref[:, o1:o2])
    s5_ref[...] = _dg(x, w_ref[:, o2:o3])
    for n in range(N_BRANCH):
        c0 = o3 + n * D_MODEL
        gate_ref[:, n * D_MODEL:(n + 1) * D_MODEL] = _sigmoid(_dg(x, w_ref[:, c0:c0 + D_MODEL]))


def _in_proj(x2, w, bsz, seq, tm):
    n = bsz * seq
    nt = seq // tm
    row = lambda b, i: (b * nt + i, 0)
    return pl.pallas_call(
        _in_proj_kernel,
        grid=(bsz, nt),
        in_specs=[pl.BlockSpec((tm, D_MODEL), row), _full((D_MODEL, IN_PAD))],
        out_specs=[pl.BlockSpec((tm, MLA_PAD), row),
                   pl.BlockSpec((tm, RWKV_IN), row),
                   pl.BlockSpec((tm, S5_WIDTH), lambda b, i: (i, b)),
                   pl.BlockSpec((tm, N_BRANCH * D_MODEL), row)],
        out_shape=[jax.ShapeDtypeStruct((n, MLA_PAD), F32),
                   jax.ShapeDtypeStruct((n, RWKV_IN), F32),
                   jax.ShapeDtypeStruct((seq, bsz * S5_WIDTH), F32),
                   jax.ShapeDtypeStruct((n, N_BRANCH * D_MODEL), F32)],
        compiler_params=_params(("parallel", "parallel")),
        name="in_proj",
    )(x2, w)


def _prep_mla_weights(w_uq, w_ukv):
    half = MLA_ROPE // 2
    wq = w_uq.reshape(Q_LORA, MLA_HEADS, MLA_NOPE + MLA_ROPE)
    nope, pe1, pe2 = wq[..., :MLA_NOPE], wq[..., MLA_NOPE:MLA_NOPE + half], wq[..., MLA_NOPE + half:]
    z32 = jnp.zeros((Q_LORA, MLA_HEADS, LANE - MLA_NOPE - MLA_ROPE), w_uq.dtype)
    z64 = jnp.zeros((Q_LORA, MLA_HEADS, MLA_NOPE), w_uq.dtype)
    wq_main = jnp.concatenate([nope, pe1, pe2, z32], -1).reshape(Q_LORA, MLA_HEADS * LANE)
    wq_rot = jnp.concatenate([z64, -pe2, pe1, z32], -1).reshape(Q_LORA, MLA_HEADS * LANE)
    wkv = w_ukv.reshape(KV_LORA, MLA_HEADS, MLA_NOPE + MLA_V)
    k_nope, v = wkv[..., :MLA_NOPE], wkv[..., MLA_NOPE:]
    zk = jnp.zeros((KV_LORA, MLA_HEADS, LANE - MLA_NOPE), w_ukv.dtype)
    wk = jnp.concatenate([k_nope, zk], -1).reshape(KV_LORA, MLA_HEADS * LANE)
    zv = jnp.zeros_like(v)
    even = (jnp.arange(MLA_HEADS) % 2 == 0)[None, :, None]
    wv = jnp.concatenate([jnp.where(even, v, zv), jnp.where(even, zv, v)], -1).reshape(KV_LORA, MLA_HEADS * LANE)
    return wq_main.astype(BF16), wq_rot.astype(BF16), wk.astype(BF16), wv.astype(BF16)


def _mla_prep_kernel(p_ref, c_ref, s_ref, qn_ref, kvn_ref, wq_ref, wqr_ref, wk_ref, wv_ref,
                     q_out, k_out, v_out):
    cos = c_ref[...]
    sin = s_ref[...]
    qn = _rms(p_ref[:, :Q_LORA], qn_ref[...]).astype(BF16)
    kvn = _rms(p_ref[:, Q_LORA:Q_LORA + KV_LORA], kvn_ref[...]).astype(BF16)
    o = Q_LORA + KV_LORA
    k_rope = p_ref[:, o:o + LANE] * cos + p_ref[:, o + LANE:o + 2 * LANE] * sin
    scale = (MLA_NOPE + MLA_ROPE) ** -0.5 * math.log2(math.e)
    lane = lax.broadcasted_iota(jnp.int32, (1, LANE), 1)
    for h in range(MLA_HEADS):
        cols = slice(h * LANE, (h + 1) * LANE)
        q = _dg(qn, wq_ref[:, cols]) * cos + _dg(qn, wqr_ref[:, cols]) * sin
        q_out[0, h] = (q * scale).astype(BF16)
        k_out[0, h] = (_dg(kvn, wk_ref[:, cols]) + k_rope).astype(BF16)
        ones = jnp.where(lane == (1 - h % 2) * MLA_V, 1.0, 0.0)
        v_out[0, h] = (_dg(kvn, wv_ref[:, cols]) + ones).astype(BF16)


def _mla_prep(p_mla, cos, sin, q_norm, kv_norm, wq, wqr, wk, wv, bsz, seq, tm):
    nt = seq // tm
    row = lambda b, i: (b * nt + i, 0)
    hw = MLA_HEADS * LANE
    out = jax.ShapeDtypeStruct((bsz, MLA_HEADS, seq, LANE), BF16)
    ospec = pl.BlockSpec((1, MLA_HEADS, tm, LANE), lambda b, i: (b, 0, i, 0))
    return pl.pallas_call(
        _mla_prep_kernel,
        grid=(bsz, nt),
        in_specs=[pl.BlockSpec((tm, MLA_PAD), row),
                  pl.BlockSpec((tm, LANE), row), pl.BlockSpec((tm, LANE), row),
                  _full((1, Q_LORA)), _full((1, KV_LORA)),
                  _full((Q_LORA, hw)), _full((Q_LORA, hw)), _full((KV_LORA, hw)), _full((KV_LORA, hw))],
        out_specs=[ospec, ospec, ospec],
        out_shape=[out, out, out],
        compiler_params=_params(("parallel", "parallel")),
        name="mla_prep",
    )(p_mla, cos, sin, q_norm, kv_norm, wq, wqr, wk, wv)


def _flash_kernel(q_ref, k_ref, v_ref, o_ref, *, tq):
    i = pl.program_id(2)
    row = lax.broadcasted_iota(jnp.int32, (tq, tq), 0)
    col = lax.broadcasted_iota(jnp.int32, (tq, tq), 1)
    causal = row >= col
    lane = lax.broadcasted_iota(jnp.int32, (tq, LANE), 1)
    qs = [q_ref[0, hh] for hh in range(2)]

    def scores(hh, j):
        start = pl.multiple_of(j * tq, tq)
        kj = k_ref[0, hh, pl.ds(start, tq), :]
        vj = v_ref[0, hh, pl.ds(start, tq), :]
        return _dg(qs[hh], kj, NT), vj

    def update(carry, s, vj):
        m, acc = carry
        m_new = jnp.maximum(m, jnp.max(s, axis=-1, keepdims=True))
        p = jnp.exp2(s - m_new)
        acc = jnp.exp2(m - m_new) * acc + _dg(p.astype(BF16), vj)
        return m_new, acc

    def body(j, carry):
        return tuple(update(carry[hh], *scores(hh, j)) for hh in range(2))

    init = (jnp.full((tq, 1), NEG_INF, F32), jnp.zeros((tq, LANE), F32))
    carry = lax.fori_loop(0, i, body, (init, init))
    out = None
    for hh in range(2):
        s, vj = scores(hh, i)
        _, acc = update(carry[hh], jnp.where(causal, s, NEG_INF), vj)
        ones_lane = (1 - hh) * MLA_V
        mine = (lane >= hh * MLA_V) & (lane < (hh + 1) * MLA_V)
        l = jnp.sum(jnp.where(lane == ones_lane, acc, 0.0), axis=-1, keepdims=True)
        o = jnp.where(mine, acc / l, 0.0)
        out = o if out is None else out + o
    o_ref[...] = out


def _flash(q, k, v, bsz, seq, tq):
    nt = seq // tq
    return pl.pallas_call(
        functools.partial(_flash_kernel, tq=tq),
        grid=(bsz, MLA_HEADS // 2, nt),
        in_specs=[pl.BlockSpec((1, 2, tq, LANE), lambda b, h, i: (b, h, i, 0)),
                  pl.BlockSpec((1, 2, seq, LANE), lambda b, h, i: (b, h, 0, 0)),
                  pl.BlockSpec((1, 2, seq, LANE), lambda b, h, i: (b, h, 0, 0))],
        out_specs=pl.BlockSpec((tq, LANE), lambda b, h, i: (b * nt + i, h)),
        out_shape=jax.ShapeDtypeStruct((bsz * seq, MLA_HEADS * MLA_V), F32),
        compiler_params=_params(("parallel", "parallel", "arbitrary")),
        name="mla_flash",
    )(q, k, v)


def _rwkv_prep_kernel(*refs, tm, has_vres):
    if has_vres:
        (p_ref, prev_ref, vf_ref, mu_ref, w0_ref, wup_ref, a0_ref, aup_ref, gup_ref, kk_ref, ka_ref,
         rk_ref, vd_ref, vu_ref, vb_ref, tri_ref, blk_ref, sel_ref, bd_ref,
         rh_o, ah_o, bh_o, kh_o, bt_o, kt_o, v_o, bon_o, g_o, gam_o) = refs
    else:
        (p_ref, prev_ref, mu_ref, w0_ref, wup_ref, a0_ref, aup_ref, gup_ref, kk_ref, ka_ref,
         rk_ref, tri_ref, blk_ref, sel_ref, bd_ref,
         rh_o, ah_o, bh_o, kh_o, bt_o, kt_o, v_o, bon_o, g_o, gam_o) = refs
    i = pl.program_id(1)
    p = p_ref[...]
    first = jnp.where(i == 0, 0.0, prev_ref[7:8, :])
    rows = lax.broadcasted_iota(jnp.int32, p.shape, 0)
    prev = jnp.where(rows == 0, first, pltpu.roll(p, 1, axis=0))
    ps = p + (prev - p) * mu_ref[...]
    W = RWKV_WIDTH
    r, k, v = ps[:, :W], ps[:, W:2 * W], ps[:, 2 * W:3 * W]
    lora = ps[:, 3 * W:3 * W + DECAY_LORA + ICLR_LORA]
    g_d = ps[:, 3 * W + DECAY_LORA + ICLR_LORA:]
    w_log = -_softplus(-(w0_ref[...] + _dot(jnp.tanh(lora), wup_ref[...]))) - 0.5
    lw = -jnp.exp(w_log)
    iclr = _sigmoid(a0_ref[...] + _dot(lora, aup_ref[...]))
    g_o[...] = _dot(_sigmoid(g_d), gup_ref[...])
    if has_vres:
        mix = _sigmoid(vb_ref[...] + _dot(_dot(v, vd_ref[...]), vu_ref[...]))
        v = v + (vf_ref[...] - v) * mix
    bd = bd_ref[...]
    kk = k * kk_ref[...]
    kk = kk / jnp.maximum(jnp.sqrt(_dot_exact_rhs(kk * kk, bd)), 1e-12)
    k_h = k * (1.0 + (iclr - 1.0) * ka_ref[...])
    bon_o[...] = _dot_exact_rhs(r * k_h * rk_ref[...], bd) * v
    v_o[...] = v
    cum = _dot_exact_lhs(tri_ref[...], lw)
    tot = _dot_exact_lhs(blk_ref[...], lw)
    b = kk * iclr
    e_neg = jnp.exp(-cum)
    e_rem = jnp.exp(tot - cum)
    rh_o[...] = r * jnp.exp(cum)
    ah_o[...] = -kk * jnp.exp(cum - lw)
    bh_o[...] = b * e_neg
    kh_o[...] = k_h * e_neg
    bt_o[...] = b * e_rem
    kt_o[...] = k_h * e_rem
    gam_o[...] = jnp.exp(_dot_exact_lhs(sel_ref[...], lw))


def _rwkv_prep(p_rwkv, v_first, wts, consts, bsz, seq, tm):
    n = bsz * seq
    nt = seq // tm
    has_vres = v_first is not None
    row = lambda b, i: (b * nt + i, 0)
    prev_map = lambda b, i: (jnp.maximum((b * nt + i) * (tm // 8) - 1, 0), 0)
    wide = pl.BlockSpec((tm, RWKV_WIDTH), row)
    ins = [p_rwkv, p_rwkv]
    specs = [pl.BlockSpec((tm, RWKV_IN), row), pl.BlockSpec((8, RWKV_IN), prev_map)]
    if has_vres:
        ins.append(v_first)
        specs.append(wide)
    names = ["mu", "w0", "wup", "a0", "aup", "gup", "kk", "ka", "rk"]
    if has_vres:
        names += ["vd", "vu", "vb"]
    for nm in names:
        ins.append(wts[nm])
        specs.append(_full(wts[nm].shape))
    for c in consts:
        ins.append(c)
        specs.append(_full(c.shape))
    o = jax.ShapeDtypeStruct((n, RWKV_WIDTH), F32)
    nch = tm // CHUNK
    return pl.pallas_call(
        functools.partial(_rwkv_prep_kernel, tm=tm, has_vres=has_vres),
        grid=(bsz, nt),
        in_specs=specs,
        out_specs=[wide] * 9 + [pl.BlockSpec((nch, RWKV_WIDTH), lambda b, i: (b * nt + i, 0))],
        out_shape=[o] * 9 + [jax.ShapeDtypeStruct((n // CHUNK, RWKV_WIDTH), F32)],
        compiler_params=_params(("parallel", "parallel")),
        name="rwkv_prep",
    )(*ins)


def _rwkv_scan_kernel(rh_ref, ah_ref, bh_ref, kh_ref, bt_ref, kt_ref, v_ref, bon_ref, g_ref, gam_ref,
                      lng_ref, lnb_ref, o_ref, s_ref, *, nch):
    c = pl.program_id(1)

    @pl.when(c == 0)
    def _():
        s_ref[...] = jnp.zeros_like(s_ref)

    C = CHUNK
    rr = lax.broadcasted_iota(jnp.int32, (C, C), 0)
    cc = lax.broadcasted_iota(jnp.int32, (C, C), 1)
    strict = rr > cc
    incl = rr >= cc
    eye = jnp.where(rr == cc, 1.0, 0.0).astype(F32)
    heads = range(RWKV_HEADS)
    sls = [slice(h * RWKV_HEAD, (h + 1) * RWKV_HEAD) for h in heads]
    pre = []
    for ci in range(nch):
        rows = slice(ci * C, (ci + 1) * C)
        ar = [jnp.concatenate([ah_ref[rows, sl], rh_ref[rows, sl]], axis=0) for sl in sls]
        bk = [jnp.concatenate([bh_ref[rows, sl], kh_ref[rows, sl]], axis=0) for sl in sls]
        gm = [_dot(ar[h], bk[h], NT) for h in heads]
        a_ab = [jnp.where(strict, g[:C, :C], 0.0) for g in gm]
        a_ak = [jnp.where(strict, g[:C, C:], 0.0) for g in gm]
        a_r = [jnp.concatenate([jnp.where(incl, g[C:, :C], 0.0), jnp.where(incl, g[C:, C:], 0.0)], axis=1)
               for g in gm]
        pre.append((ar, a_ab, a_ak, a_r))
    pw = [p[1] for p in pre]
    tinv = [[eye + a for a in p[1]] for p in pre]
    for _ in range(int(math.log2(C)) - 1):
        pw = [[_dot(p, p) for p in prow] for prow in pw]
        tinv = [[t + _dot(p, t) for p, t in zip(prow, trow)] for prow, trow in zip(pw, tinv)]
    s = [s_ref[h] for h in heads]
    for ci in range(nch):
        rows = slice(ci * C, (ci + 1) * C)
        ar, _, a_ak, a_r = pre[ci]
        gam = gam_ref[pl.ds((c * nch) % 8 + ci, 1), :]
        vv = [v_ref[rows, sl] for sl in sls]
        xs = [_dot(ar[h], s[h], NT) for h in heads]
        z = [xs[h][:C] + _dot(a_ak[h], vv[h]) for h in heads]
        u = [_dot(tinv[ci][h], z[h]) for h in heads]
        uv = [jnp.concatenate([u[h], vv[h]], axis=0) for h in heads]
        o = [xs[h][C:] + _dot(a_r[h], uv[h]) for h in heads]
        btk = [jnp.concatenate([bt_ref[rows, sl], kt_ref[rows, sl]], axis=0) for sl in sls]
        s = [s[h] * gam[:, sls[h]] + _dot(uv[h], btk[h], TN) for h in heads]
        for h, sl in enumerate(sls):
            oc = o[h] - jnp.mean(o[h], axis=-1, keepdims=True)
            on = oc * lax.rsqrt(jnp.mean(oc * oc, axis=-1, keepdims=True) + RWKV_GN_EPS)
            y = on * lng_ref[:, sl] + lnb_ref[:, sl]
            o_ref[rows, sl] = (y + bon_ref[rows, sl]) * g_ref[rows, sl]
    for h in heads:
        s_ref[h] = s[h]


def _rwkv_scan(prep, ln_g, ln_b, bsz, seq, nch):
    n = bsz * seq
    tb = nch * CHUNK
    nc = seq // tb
    row = lambda b, c: (b * nc + c, 0)
    wide = pl.BlockSpec((tb, RWKV_WIDTH), row)
    return pl.pallas_call(
        functools.partial(_rwkv_scan_kernel, nch=nch),
        grid=(bsz, nc),
        in_specs=[wide] * 9 + [pl.BlockSpec((8, RWKV_WIDTH), lambda b, c: ((b * nc + c) * nch // 8, 0)),
                               _full((1, RWKV_WIDTH)), _full((1, RWKV_WIDTH))],
        out_specs=wide,
        out_shape=jax.ShapeDtypeStruct((n, RWKV_WIDTH), F32),
        scratch_shapes=[pltpu.VMEM((RWKV_HEADS, RWKV_HEAD, RWKV_HEAD), F32)],
        compiler_params=_params(("parallel", "arbitrary")),
        name="rwkv_scan",
    )(*prep, ln_g, ln_b)


def _rwkv_consts(tm):
    t = jnp.arange(tm)
    same = (t[:, None] // CHUNK) == (t[None, :] // CHUNK)
    tri = (same & (t[:, None] >= t[None, :])).astype(BF16)
    blk = same.astype(BF16)
    sel = ((jnp.arange(tm // CHUNK)[:, None]) == (t[None, :] // CHUNK)).astype(BF16)
    ch = jnp.arange(RWKV_WIDTH) // RWKV_HEAD
    bd = (ch[:, None] == ch[None, :]).astype(BF16)
    return tri, blk, sel, bd


def _prep_s5(a_re, a_im, log_dt, b_re, b_im, c_re, c_im, d, bsz):
    dt = jnp.exp(log_dt.astype(F32))[:, None]
    lam_re = jnp.minimum(a_re.astype(F32), -1e-4)
    lam_im = a_im.astype(F32)
    mag = jnp.exp(dt * lam_re)
    ab_re, ab_im = mag * jnp.cos(dt * lam_im), mag * jnp.sin(dt * lam_im)
    den = lam_re * lam_re + lam_im * lam_im
    f_re = ((ab_re - 1.0) * lam_re + ab_im * lam_im) / den
    f_im = (ab_im * lam_re - (ab_re - 1.0) * lam_im) / den
    bb_re = f_re[..., None] * b_re - f_im[..., None] * b_im
    bb_im = f_re[..., None] * b_im + f_im[..., None] * b_re
    eye = jnp.eye(S5_GROUPS, dtype=F32)

    def expand_b(bb):
        return jnp.einsum('gpc,gh->gchp', bb, eye).reshape(S5_WIDTH, S5_LANES)

    def expand_c(cm):
        return jnp.einsum('gcp,gh->gphc', cm.astype(F32), eye).reshape(S5_LANES, S5_WIDTH)

    tile = lambda a: jnp.broadcast_to(a.reshape(1, S5_LANES), (bsz, S5_LANES))
    return (expand_b(bb_re).astype(BF16), expand_b(bb_im).astype(BF16), tile(ab_re), tile(ab_im),
            expand_c(c_re).astype(BF16), expand_c(c_im).astype(BF16), d.astype(F32).reshape(1, S5_WIDTH))


def _s5_kernel(u_ref, bre_ref, bim_ref, are_ref, aim_ref, cre_ref, cim_ref, d_ref, wglu_ref, bglu_ref,
               o_ref, xre_s, xim_s, st_re, st_im, *, ts, bsz):
    @pl.when(pl.program_id(0) == 0)
    def _():
        st_re[...] = jnp.zeros_like(st_re)
        st_im[...] = jnp.zeros_like(st_im)

    u = u_ref[...]
    ub = u.astype(BF16)
    xre_s[...] = _dg(ub, bre_ref[...])
    xim_s[...] = _dg(ub, bim_ref[...])
    def step(t, carry):
        x_re, x_im = carry
        a_re = are_ref[...]
        a_im = aim_ref[...]
        r0 = pl.multiple_of(t * bsz, bsz)
        n_re = a_re * x_re - a_im * x_im + xre_s[pl.ds(r0, bsz), :]
        n_im = a_re * x_im + a_im * x_re + xim_s[pl.ds(r0, bsz), :]
        xre_s[pl.ds(r0, bsz), :] = n_re
        xim_s[pl.ds(r0, bsz), :] = n_im
        return n_re, n_im

    x_re, x_im = lax.fori_loop(0, ts, step, (st_re[...], st_im[...]))
    st_re[...] = x_re
    st_im[...] = x_im
    y = _dot(xre_s[...], cre_ref[...]) - _dot(xim_s[...], cim_ref[...]) + d_ref[...] * u
    y = _gelu_tanh(y)
    o_ref[...] = y * _sigmoid(_dot(y, wglu_ref[...]) + bglu_ref[...])


def _s5(u_tm, s5w, w_glu, b_glu, bsz, seq, ts):
    rows = ts * bsz
    bre, bim, are, aim, cre, cim, d = s5w
    tile = pl.BlockSpec((rows, S5_WIDTH), lambda i: (i, 0))
    return pl.pallas_call(
        functools.partial(_s5_kernel, ts=ts, bsz=bsz),
        grid=(seq // ts,),
        in_specs=[tile, _full(bre.shape), _full(bim.shape), _full(are.shape), _full(aim.shape),
                  _full(cre.shape), _full(cim.shape), _full(d.shape), _full(w_glu.shape), _full(b_glu.shape)],
        out_specs=tile,
        out_shape=jax.ShapeDtypeStruct((seq * bsz, S5_WIDTH), F32),
        scratch_shapes=[pltpu.VMEM((rows, S5_LANES), F32), pltpu.VMEM((rows, S5_LANES), F32),
                        pltpu.VMEM((bsz, S5_LANES), F32), pltpu.VMEM((bsz, S5_LANES), F32)],
        compiler_params=_params(("arbitrary",)),
        name="s5_scan",
    )(u_tm, bre, bim, are, aim, cre, cim, d, w_glu, b_glu)


def _merge_kernel(*refs, with_router):
    if with_router:
        x_ref, ym_ref, yr_ref, ys_ref, gt_ref, wb_ref, wo_ref, g_ref, b_ref, rt_ref, o_ref, lg_ref = refs
    else:
        x_ref, ym_ref, yr_ref, ys_ref, gt_ref, wb_ref, wo_ref, g_ref, b_ref, o_ref = refs
    mixed = None
    for n, y_ref in enumerate((ym_ref, yr_ref, ys_ref)):
        t = gt_ref[:, n * D_MODEL:(n + 1) * D_MODEL] * _dot(y_ref[...], wb_ref[n])
        mixed = t if mixed is None else mixed + t
    z = DEEPNORM_ALPHA * x_ref[...] + _dot(mixed, wo_ref[...])
    out = _layer_norm(z, g_ref[...], b_ref[...])
    o_ref[...] = out
    if with_router:
        lg_ref[...] = _dot3(out, rt_ref[...])


def _merge(x2, y_mla, y_rwkv, y_s5_tm, gates, wb, wo, g, b, router, bsz, seq, tm):
    n = bsz * seq
    nt = seq // tm
    row = lambda bb, i: (bb * nt + i, 0)
    with_router = router is not None
    ins = [x2, y_mla, y_rwkv, y_s5_tm, gates, wb, wo, g, b]
    specs = [pl.BlockSpec((tm, D_MODEL), row),
             pl.BlockSpec((tm, 512), row), pl.BlockSpec((tm, 512), row),
             pl.BlockSpec((tm, S5_WIDTH), lambda bb, i: (i, bb)),
             pl.BlockSpec((tm, N_BRANCH * D_MODEL), row),
             _full(wb.shape), _full(wo.shape), _full(g.shape), _full(b.shape)]
    out_specs = [pl.BlockSpec((tm, D_MODEL), row)]
    out_shape = [jax.ShapeDtypeStruct((n, D_MODEL), F32)]
    if with_router:
        ins.append(router)
        specs.append(_full(router.shape))
        out_specs.append(pl.BlockSpec((tm, LANE), row))
        out_shape.append(jax.ShapeDtypeStruct((n, LANE), F32))
    res = pl.pallas_call(
        functools.partial(_merge_kernel, with_router=with_router),
        grid=(bsz, nt),
        in_specs=specs,
        out_specs=out_specs,
        out_shape=out_shape,
        compiler_params=_params(("parallel", "parallel")),
        name="merge_ln1",
    )(*ins)
    return res if with_router else (res[0], None)


def _ffn_kernel(x_ref, wg_ref, wu_ref, wd_ref, g_ref, b_ref, o_ref, acc_ref):
    j = pl.program_id(1)
    x = x_ref[...]
    xb = x.astype(BF16)
    h = _silu(_dg(xb, wg_ref[...])) * _dg(xb, wu_ref[...])
    part = _dot(h, wd_ref[...])

    @pl.when(j == 0)
    def _():
        acc_ref[...] = part

    @pl.when(j > 0)
    def _():
        acc_ref[...] += part

    @pl.when(j == pl.num_programs(1) - 1)
    def _():
        o_ref[...] = _layer_norm(DEEPNORM_ALPHA * x + acc_ref[...], g_ref[...], b_ref[...])


def _ffn(x2, wg, wu, wd, g, b, tm, tf):
    n = x2.shape[0]
    return pl.pallas_call(
        _ffn_kernel,
        grid=(n // tm, FFN_DIM // tf),
        in_specs=[pl.BlockSpec((tm, D_MODEL), lambda i, j: (i, 0)),
                  pl.BlockSpec((D_MODEL, tf), lambda i, j: (0, j)),
                  pl.BlockSpec((D_MODEL, tf), lambda i, j: (0, j)),
                  pl.BlockSpec((tf, D_MODEL), lambda i, j: (j, 0)),
                  _full(g.shape), _full(b.shape)],
        out_specs=pl.BlockSpec((tm, D_MODEL), lambda i, j: (i, 0)),
        out_shape=jax.ShapeDtypeStruct((n, D_MODEL), F32),
        scratch_shapes=[pltpu.VMEM((tm, D_MODEL), F32)],
        compiler_params=_params(("parallel", "arbitrary")),
        name="ffn_ln2",
    )(x2, wg, wu, wd, g, b)


def _moe_kernel(be_ref, nu_ref, xs_ref, sw_ref, wg_ref, wu_ref, wd_ref, o_ref, acc_ref):
    i = pl.program_id(0)
    j = pl.program_id(1)

    @pl.when(i < nu_ref[0])
    def _():
        xb = xs_ref[...]
        h = _silu(_dg(xb, wg_ref[0])) * _dg(xb, wu_ref[0])
        part = _dot(h, wd_ref[0])

        @pl.when(j == 0)
        def _():
            acc_ref[...] = part

        @pl.when(j > 0)
        def _():
            acc_ref[...] += part

        @pl.when(j == pl.num_programs(1) - 1)
        def _():
            o_ref[...] = acc_ref[...] * sw_ref[...]


def _moe_experts(xs, slot_w, block_e, n_used, wg, wu, wd, tf):
    n_slots = xs.shape[0]
    n_blocks = n_slots // MOE_BLOCK
    nf = EXPERT_DIM // tf

    def blk(i, nu):
        return jnp.minimum(i, nu[0] - 1)

    def fcol(i, j, nu):
        return jnp.where(i < nu[0], j, nf - 1)

    grid_spec = pltpu.PrefetchScalarGridSpec(
        num_scalar_prefetch=2,
        grid=(n_blocks, nf),
        in_specs=[pl.BlockSpec((MOE_BLOCK, D_MODEL), lambda i, j, be, nu: (blk(i, nu), 0)),
                  pl.BlockSpec((MOE_BLOCK, 1), lambda i, j, be, nu: (blk(i, nu), 0)),
                  pl.BlockSpec((1, D_MODEL, tf), lambda i, j, be, nu: (be[blk(i, nu)], 0, fcol(i, j, nu))),
                  pl.BlockSpec((1, D_MODEL, tf), lambda i, j, be, nu: (be[blk(i, nu)], 0, fcol(i, j, nu))),
                  pl.BlockSpec((1, tf, D_MODEL), lambda i, j, be, nu: (be[blk(i, nu)], fcol(i, j, nu), 0))],
        out_specs=pl.BlockSpec((MOE_BLOCK, D_MODEL), lambda i, j, be, nu: (blk(i, nu), 0)),
        scratch_shapes=[pltpu.VMEM((MOE_BLOCK, D_MODEL), F32)],
    )
    return pl.pallas_call(
        _moe_kernel,
        grid_spec=grid_spec,
        out_shape=jax.ShapeDtypeStruct((n_slots, D_MODEL), F32),
        compiler_params=_params(("arbitrary", "arbitrary")),
        name="moe_experts",
    )(block_e, n_used, xs, slot_w, wg, wu, wd)


def _combine_kernel(x_ref, y0_ref, y1_ref, g_ref, b_ref, o_ref):
    z = DEEPNORM_ALPHA * x_ref[...] + (y0_ref[...] + y1_ref[...])
    o_ref[...] = _layer_norm(z, g_ref[...], b_ref[...])


def _combine(x2, y0, y1, g, b, tm):
    n = x2.shape[0]
    tile = pl.BlockSpec((tm, D_MODEL), lambda i: (i, 0))
    return pl.pallas_call(
        _combine_kernel,
        grid=(n // tm,),
        in_specs=[tile, tile, tile, _full(g.shape), _full(b.shape)],
        out_specs=tile,
        out_shape=jax.ShapeDtypeStruct((n, D_MODEL), F32),
        compiler_params=_params(("parallel",)),
        name="moe_combine_ln2",
    )(x2, y0, y1, g, b)


def _moe(h2, logits, wg, wu, wd, g, b, tm, tf):
    n_tok = h2.shape[0]
    n_pair = n_tok * TOP_K
    top_logit, top_e = lax.top_k(logits[:, :N_EXPERTS], TOP_K)
    top_w = jax.nn.softmax(top_logit, axis=-1)
    pair_e = top_e.reshape(-1).astype(jnp.int32)
    onehot = (pair_e[:, None] == jnp.arange(N_EXPERTS, dtype=jnp.int32)[None, :]).astype(jnp.int32)
    csum = jnp.cumsum(onehot, axis=0)
    counts = csum[-1]
    rank = jnp.sum(csum * onehot, axis=1) - 1
    start = jnp.cumsum(counts) - counts
    padded = (counts + MOE_BLOCK - 1) // MOE_BLOCK * MOE_BLOCK
    pad_end = jnp.cumsum(padded)
    pad_start = pad_end - padded
    pair_slot = pad_start[pair_e] + rank
    n_blocks = n_pair // MOE_BLOCK + N_EXPERTS
    n_slots = n_blocks * MOE_BLOCK
    block_e = jnp.minimum(jnp.searchsorted(pad_end, jnp.arange(n_blocks, dtype=jnp.int32) * MOE_BLOCK,
                                           side='right'), N_EXPERTS - 1).astype(jnp.int32)
    n_used = (pad_end[-1] // MOE_BLOCK).astype(jnp.int32).reshape(1)
    order = jnp.argsort(pair_e, stable=True).astype(jnp.int32)
    slot = jnp.arange(n_slots, dtype=jnp.int32)
    slot_e = block_e[slot // MOE_BLOCK]
    off = slot - pad_start[slot_e]
    valid = off < counts[slot_e]
    src = order[jnp.clip(start[slot_e] + off, 0, n_pair - 1)]
    slot_tok = jnp.where(valid, src // TOP_K, 0)
    slot_w = jnp.where(valid, top_w.reshape(-1)[src], 0.0)
    xs = jnp.take(h2.astype(BF16), slot_tok, axis=0)
    y = _moe_experts(xs, slot_w[:, None], block_e, n_used, wg, wu, wd, tf)
    ps = pair_slot.reshape(n_tok, TOP_K)
    y0 = jnp.take(y, ps[:, 0], axis=0)
    y1 = jnp.take(y, ps[:, 1], axis=0)
    return _combine(h2, y0, y1, g, b, tm)


def kernel(x, positions, w_in, mla_q_norm, mla_w_uq, mla_kv_norm, mla_w_ukv,
           rwkv_mu, rwkv_w0, rwkv_w_up, rwkv_a0, rwkv_a_up, rwkv_g_up, rwkv_k_k, rwkv_k_a,
           rwkv_r_k, rwkv_ln_g, rwkv_ln_b, rwkv_vres_down, rwkv_vres_up, rwkv_vres_b,
           s5_a_re, s5_a_im, s5_log_dt, s5_b_re, s5_b_im, s5_c_re, s5_c_im, s5_d, s5_w_glu, s5_b_glu,
           w_branch, w_out, ln1_g, ln1_b, ln2_g, ln2_b,
           ffn_w_gate, ffn_w_up, ffn_w_down, moe_router, moe_w_gate, moe_w_up, moe_w_down):
    bsz, seq, _ = x.shape
    n = bsz * seq
    depth = w_in.shape[0]
    tm = min(512, seq)
    tq = min(512, seq)
    ts = min(64, seq)
    row = lambda a: a.reshape(1, -1)

    cos, sin = _rope_tables(positions, tm)
    w_in_p = _prep_w_in(w_in)
    consts = _rwkv_consts(tm)
    zpad = jnp.zeros((DECAY_LORA, RWKV_WIDTH), F32)

    h = x.reshape(n, D_MODEL)
    v_first = None
    for l in range(depth):
        p_mla, p_rwkv, u_s5, gates = _in_proj(h, w_in_p[l], bsz, seq, tm)

        wq, wqr, wk, wv = _prep_mla_weights(mla_w_uq[l], mla_w_ukv[l])
        q, k, v = _mla_prep(p_mla, cos, sin, row(mla_q_norm[l]), row(mla_kv_norm[l]),
                            wq, wqr, wk, wv, bsz, seq, tm)
        y_mla = _flash(q, k, v, bsz, seq, tq)

        wts = {
            "mu": row(rwkv_mu[l]), "w0": row(rwkv_w0[l]),
            "wup": jnp.concatenate([rwkv_w_up[l], zpad], 0).astype(BF16),
            "a0": row(rwkv_a0[l]),
            "aup": jnp.concatenate([zpad, rwkv_a_up[l]], 0).astype(BF16),
            "gup": rwkv_g_up[l].astype(BF16),
            "kk": row(rwkv_k_k[l]), "ka": row(rwkv_k_a[l]), "rk": row(rwkv_r_k[l]),
        }
        if l > 0:
            wts["vd"] = rwkv_vres_down[l - 1].astype(BF16)
            wts["vu"] = rwkv_vres_up[l - 1].astype(BF16)
            wts["vb"] = row(rwkv_vres_b[l - 1])
        prep = _rwkv_prep(p_rwkv, v_first, wts, consts, bsz, seq, tm)
        if l == 0:
            v_first = prep[6]
        y_rwkv = _rwkv_scan(prep, row(rwkv_ln_g[l]), row(rwkv_ln_b[l]), bsz, seq, RWKV_NCH)

        s5w = _prep_s5(s5_a_re[l], s5_a_im[l], s5_log_dt[l], s5_b_re[l], s5_b_im[l],
                       s5_c_re[l], s5_c_im[l], s5_d[l], bsz)
        y_s5 = _s5(u_s5.reshape(seq * bsz, S5_WIDTH), s5w, s5_w_glu[l].astype(BF16), row(s5_b_glu[l]),
                   bsz, seq, ts)

        moe_layer = l % 2 == 1
        router = None
        if moe_layer:
            router = jnp.pad(moe_router[l // 2], ((0, 0), (0, LANE - N_EXPERTS)))
        h, logits = _merge(h, y_mla, y_rwkv, y_s5.reshape(seq, bsz * S5_WIDTH), gates,
                           w_branch[l].astype(BF16), w_out[l].astype(BF16), row(ln1_g[l]), row(ln1_b[l]),
                           router, bsz, seq, tm)
        if moe_layer:
            h = _moe(h, logits, moe_w_gate[l // 2].astype(BF16), moe_w_up[l // 2].astype(BF16),
                     moe_w_down[l // 2].astype(BF16), row(ln2_g[l]), row(ln2_b[l]), tm, tf=896)
        else:
            h = _ffn(h, ffn_w_gate[l // 2].astype(BF16), ffn_w_up[l // 2].astype(BF16),
                     ffn_w_down[l // 2].astype(BF16), row(ln2_g[l]), row(ln2_b[l]), tm, tf=1408)
    return h.reshape(bsz, seq, D_MODEL)
```

```python
import functools
import math

import jax
import jax.numpy as jnp
from jax import lax
from jax.experimental import pallas as pl
from jax.experimental.pallas import tpu as pltpu

F32 = jnp.float32
BF16 = jnp.bfloat16

D_MODEL = 1024
MLA_HEADS = 8
MLA_NOPE = 64
MLA_ROPE = 32
MLA_V = 64
Q_LORA = 384
KV_LORA = 256
ROPE_THETA = 10000.0
NEG_INF = -1e30
RWKV_HEADS = 8
RWKV_HEAD = 64
RWKV_WIDTH = 512
DECAY_LORA = 64
ICLR_LORA = 64
GATE_LORA = 128
RWKV_GN_EPS = 64e-5
RWKV_IN = 3 * RWKV_WIDTH + DECAY_LORA + ICLR_LORA + GATE_LORA
S5_GROUP = 16
S5_GROUPS = 32
S5_WIDTH = 512
S5_STATE = 64
S5_LANES = S5_GROUPS * S5_STATE
N_BRANCH = 3
FFN_DIM = 2816
N_EXPERTS = 8
TOP_K = 2
EXPERT_DIM = 3584
MOE_BLOCK = 512
DEPTH = 4
DEEPNORM_ALPHA = (2 * DEPTH) ** 0.25

LANE = 128
MLA_PAD = 896
CHUNK = 64
FLASH_TQ = 1024
FLASH_TD = 512
RWKV_NCH = 4
VMEM_LIMIT = 56 * 1024 * 1024

NN = ((1,), (0,))
NT = ((1,), (1,))
TN = ((0,), (0,))


def _dg(a, b, dims=NN):
    return lax.dot_general(a, b, (dims, ((), ())), preferred_element_type=F32)


def _dot(a, b, dims=NN):
    return _dg(a.astype(BF16), b.astype(BF16), dims)


def _split2(a):
    hi = a.astype(BF16)
    lo = (a - hi.astype(F32)).astype(BF16)
    return hi, lo


def _split3(a):
    hi = a.astype(BF16)
    r1 = a - hi.astype(F32)
    mid = r1.astype(BF16)
    lo = (r1 - mid.astype(F32)).astype(BF16)
    return hi, mid, lo


def _dot3(a, b, dims=NN):
    ah, al = _split2(a)
    bh, bl = _split2(b)
    return _dg(ah, bh, dims) + (_dg(ah, bl, dims) + _dg(al, bh, dims))


def _dot_exact_lhs(c_bf16, x):
    hi, mid, lo = _split3(x)
    return _dg(c_bf16, hi) + (_dg(c_bf16, mid) + _dg(c_bf16, lo))


def _dot_exact_rhs(x, c_bf16):
    hi, mid, lo = _split3(x)
    return _dg(hi, c_bf16) + (_dg(mid, c_bf16) + _dg(lo, c_bf16))


def _layer_norm(z, g, b, eps=1e-5):
    zc = z - jnp.mean(z, axis=-1, keepdims=True)
    var = jnp.mean(zc * zc, axis=-1, keepdims=True)
    return zc * lax.rsqrt(var + eps) * g + b


def _rms(x, g, eps=1e-6):
    return x * lax.rsqrt(jnp.mean(x * x, axis=-1, keepdims=True) + eps) * g


def _sigmoid(x):
    return 1.0 / (1.0 + jnp.exp(-x))


def _silu(x):
    return x * _sigmoid(x)


def _softplus(z):
    return jnp.maximum(z, 0.0) + jnp.log(1.0 + jnp.exp(-jnp.abs(z)))


def _gelu_tanh(x):
    c = math.sqrt(2.0 / math.pi)
    return x * (0.5 * (1.0 + jnp.tanh(c * (x + 0.044715 * (x * x * x)))))


def _params(sem, vmem=VMEM_LIMIT):
    return pltpu.CompilerParams(dimension_semantics=sem, vmem_limit_bytes=vmem)


def _full(shape):
    nd = len(shape)
    return pl.BlockSpec(shape, lambda *_: (0,) * nd, pipeline_mode=pl.Buffered(1))


def _rope_kernel(pos_ref, freq_ref, c_ref, s_ref):
    ang = pos_ref[...].astype(F32) * freq_ref[...]
    lane = lax.broadcasted_iota(jnp.int32, ang.shape, 1)
    c_ref[...] = jnp.where(lane < MLA_NOPE + MLA_ROPE, jnp.cos(ang), 0.0)
    s_ref[...] = jnp.sin(ang)


def _rope_tables(positions, tm):
    n = positions.size
    half = MLA_ROPE // 2
    inv_freq = ROPE_THETA ** (-jnp.arange(0, MLA_ROPE, 2, dtype=F32) / MLA_ROPE)
    freq = jnp.concatenate([jnp.zeros((MLA_NOPE,), F32), inv_freq, inv_freq,
                            jnp.zeros((LANE - MLA_NOPE - 2 * half,), F32)])[None, :]
    out = jax.ShapeDtypeStruct((n, LANE), F32)
    return pl.pallas_call(
        _rope_kernel,
        grid=(n // tm,),
        in_specs=[pl.BlockSpec((tm, 1), lambda i: (i, 0)), _full((1, LANE))],
        out_specs=[pl.BlockSpec((tm, LANE), lambda i: (i, 0))] * 2,
        out_shape=[out, out],
        compiler_params=_params(("parallel",)),
        name="rope_tables",
    )(positions.reshape(n, 1), freq)


IN_SPLITS = (MLA_PAD, RWKV_IN, S5_WIDTH, N_BRANCH * D_MODEL)
IN_PAD = sum(IN_SPLITS)


def _prep_w_in(w):
    cq = w[..., :Q_LORA]
    ckv = w[..., Q_LORA:Q_LORA + KV_LORA]
    kpe = w[..., Q_LORA + KV_LORA:Q_LORA + KV_LORA + MLA_ROPE]
    half = MLA_ROPE // 2
    k1, k2 = kpe[..., :half], kpe[..., half:]
    z64 = jnp.zeros(w.shape[:-1] + (MLA_NOPE,), w.dtype)
    z32 = jnp.zeros(w.shape[:-1] + (LANE - MLA_NOPE - MLA_ROPE,), w.dtype)
    rest = w[..., Q_LORA + KV_LORA + MLA_ROPE:]
    return jnp.concatenate([cq, ckv, z64, k1, k2, z32, z64, -k2, k1, z32, rest], axis=-1).astype(BF16)


def _in_proj_kernel(x_ref, w_ref, mla_ref, rwkv_ref, s5_ref, gate_ref):
    x = x_ref[...].astype(BF16)
    o0 = 0
    o1 = o0 + MLA_PAD
    o2 = o1 + RWKV_IN
    o3 = o2 + S5_WIDTH
    mla_ref[...] = _dg(x, w_ref[:, o0:o1])
    rwkv_ref[...] = _dg(x, w_ref[:, o1:o2])
    s5_ref[...] = _dg(x, w_ref[:, o2:o3])
    for n in range(N_BRANCH):
        c0 = o3 + n * D_MODEL
        gate_ref[:, n * D_MODEL:(n + 1) * D_MODEL] = _sigmoid(_dg(x, w_ref[:, c0:c0 + D_MODEL]))


def _in_proj(x2, w, bsz, seq, tm):
    n = bsz * seq
    nt = seq // tm
    row = lambda b, i: (b * nt + i, 0)
    return pl.pallas_call(
        _in_proj_kernel,
        grid=(bsz, nt),
        in_specs=[pl.BlockSpec((tm, D_MODEL), row), _full((D_MODEL, IN_PAD))],
        out_specs=[pl.BlockSpec((tm, MLA_PAD), row),
                   pl.BlockSpec((tm, RWKV_IN), row),
                   pl.BlockSpec((tm, S5_WIDTH), lambda b, i: (i, b)),
                   pl.BlockSpec((tm, N_BRANCH * D_MODEL), row)],
        out_shape=[jax.ShapeDtypeStruct((n, MLA_PAD), F32),
                   jax.ShapeDtypeStruct((n, RWKV_IN), F32),
                   jax.ShapeDtypeStruct((seq, bsz * S5_WIDTH), F32),
                   jax.ShapeDtypeStruct((n, N_BRANCH * D_MODEL), F32)],
        compiler_params=_params(("parallel", "parallel")),
        name="in_proj",
    )(x2, w)


def _prep_mla_weights(w_uq, w_ukv):
    half = MLA_ROPE // 2
    wq = w_uq.reshape(Q_LORA, MLA_HEADS, MLA_NOPE + MLA_ROPE)
    nope, pe1, pe2 = wq[..., :MLA_NOPE], wq[..., MLA_NOPE:MLA_NOPE + half], wq[..., MLA_NOPE + half:]
    z32 = jnp.zeros((Q_LORA, MLA_HEADS, LANE - MLA_NOPE - MLA_ROPE), w_uq.dtype)
    z64 = jnp.zeros((Q_LORA, MLA_HEADS, MLA_NOPE), w_uq.dtype)
    wq_main = jnp.concatenate([nope, pe1, pe2, z32], -1).reshape(Q_LORA, MLA_HEADS * LANE)
    wq_rot = jnp.concatenate([z64, -pe2, pe1, z32], -1).reshape(Q_LORA, MLA_HEADS * LANE)
    wkv = w_ukv.reshape(KV_LORA, MLA_HEADS, MLA_NOPE + MLA_V)
    k_nope, v = wkv[..., :MLA_NOPE], wkv[..., MLA_NOPE:]
    zk = jnp.zeros((KV_LORA, MLA_HEADS, LANE - MLA_NOPE), w_ukv.dtype)
    wk = jnp.concatenate([k_nope, zk], -1).reshape(KV_LORA, MLA_HEADS * LANE)
    zv = jnp.zeros_like(v)
    even = (jnp.arange(MLA_HEADS) % 2 == 0)[None, :, None]
    wv = jnp.concatenate([jnp.where(even, v, zv), jnp.where(even, zv, v)], -1).reshape(KV_LORA, MLA_HEADS * LANE)
    return wq_main.astype(BF16), wq_rot.astype(BF16), wk.astype(BF16), wv.astype(BF16)


def _mla_prep_kernel(p_ref, c_ref, s_ref, qn_ref, kvn_ref, wq_ref, wqr_ref, wk_ref, wv_ref,
                     q_out, k_out, v_out):
    cos = c_ref[...]
    sin = s_ref[...]
    qn = _rms(p_ref[:, :Q_LORA], qn_ref[...]).astype(BF16)
    kvn = _rms(p_ref[:, Q_LORA:Q_LORA + KV_LORA], kvn_ref[...]).astype(BF16)
    o = Q_LORA + KV_LORA
    k_rope = p_ref[:, o:o + LANE] * cos + p_ref[:, o + LANE:o + 2 * LANE] * sin
    scale = (MLA_NOPE + MLA_ROPE) ** -0.5 * math.log2(math.e)
    lane = lax.broadcasted_iota(jnp.int32, (1, LANE), 1)
    for h in range(MLA_HEADS):
        cols = slice(h * LANE, (h + 1) * LANE)
        q = _dg(qn, wq_ref[:, cols]) * cos + _dg(qn, wqr_ref[:, cols]) * sin
        q_out[0, h] = (q * scale).astype(BF16)
        k_out[0, h] = (_dg(kvn, wk_ref[:, cols]) + k_rope).astype(BF16)
        ones = jnp.where(lane == (1 - h % 2) * MLA_V, 1.0, 0.0)
        v_out[0, h] = (_dg(kvn, wv_ref[:, cols]) + ones).astype(BF16)


def _mla_prep(p_mla, cos, sin, q_norm, kv_norm, wq, wqr, wk, wv, bsz, seq, tm):
    nt = seq // tm
    row = lambda b, i: (b * nt + i, 0)
    hw = MLA_HEADS * LANE
    out = jax.ShapeDtypeStruct((bsz, MLA_HEADS, seq, LANE), BF16)
    ospec = pl.BlockSpec((1, MLA_HEADS, tm, LANE), lambda b, i: (b, 0, i, 0))
    return pl.pallas_call(
        _mla_prep_kernel,
        grid=(bsz, nt),
        in_specs=[pl.BlockSpec((tm, MLA_PAD), row),
                  pl.BlockSpec((tm, LANE), row), pl.BlockSpec((tm, LANE), row),
                  _full((1, Q_LORA)), _full((1, KV_LORA)),
                  _full((Q_LORA, hw)), _full((Q_LORA, hw)), _full((KV_LORA, hw)), _full((KV_LORA, hw))],
        out_specs=[ospec, ospec, ospec],
        out_shape=[out, out, out],
        compiler_params=_params(("parallel", "parallel")),
        name="mla_prep",
    )(p_mla, cos, sin, q_norm, kv_norm, wq, wqr, wk, wv)


def _flash_kernel(q_ref, k_ref, v_ref, o_ref, *, tq, td):
    i = pl.program_id(2)
    lane = lax.broadcasted_iota(jnp.int32, (tq, LANE), 1)
    qs = [q_ref[0, hh] for hh in range(2)]

    def scores(hh, start, size, r0=0):
        kj = k_ref[0, hh, pl.ds(start, size), :]
        vj = v_ref[0, hh, pl.ds(start, size), :]
        return _dg(qs[hh][r0:], kj, NT), vj

    def update(carry, s, vj):
        m, acc = carry
        m_new = jnp.maximum(m, jnp.max(s, axis=-1, keepdims=True))
        p = jnp.exp2(s - m_new)
        acc = jnp.exp2(m - m_new) * acc + _dg(p.astype(BF16), vj)
        return m_new, acc

    def body(j, carry):
        start = pl.multiple_of(j * tq, tq)
        return tuple(update(carry[hh], *scores(hh, start, tq)) for hh in range(2))

    init = (jnp.full((tq, 1), NEG_INF, F32), jnp.zeros((tq, LANE), F32))
    carry = lax.fori_loop(0, i, body, (init, init))
    for d in range(tq // td):
        r0 = d * td
        start = pl.multiple_of(i * tq + r0, td)
        causal = (lax.broadcasted_iota(jnp.int32, (tq - r0, td), 0)
                  >= lax.broadcasted_iota(jnp.int32, (tq - r0, td), 1))
        stepped = []
        for hh in range(2):
            m, acc = carry[hh]
            s, vj = scores(hh, start, td, r0)
            m_low, acc_low = update((m[r0:], acc[r0:]), jnp.where(causal, s, NEG_INF), vj)
            if r0:
                m_low = jnp.concatenate([m[:r0], m_low], axis=0)
                acc_low = jnp.concatenate([acc[:r0], acc_low], axis=0)
            stepped.append((m_low, acc_low))
        carry = tuple(stepped)
    out = None
    for hh in range(2):
        acc = carry[hh][1]
        ones_lane = (1 - hh) * MLA_V
        mine = (lane >= hh * MLA_V) & (lane < (hh + 1) * MLA_V)
        l = jnp.sum(jnp.where(lane == ones_lane, acc, 0.0), axis=-1, keepdims=True)
        o = jnp.where(mine, acc / l, 0.0)
        out = o if out is None else out + o
    o_ref[...] = out


def _flash(q, k, v, bsz, seq, tq, td):
    nt = seq // tq
    return pl.pallas_call(
        functools.partial(_flash_kernel, tq=tq, td=td),
        grid=(bsz, MLA_HEADS // 2, nt),
        in_specs=[pl.BlockSpec((1, 2, tq, LANE), lambda b, h, i: (b, h, i, 0)),
                  pl.BlockSpec((1, 2, seq, LANE), lambda b, h, i: (b, h, 0, 0)),
                  pl.BlockSpec((1, 2, seq, LANE), lambda b, h, i: (b, h, 0, 0))],
        out_specs=pl.BlockSpec((tq, LANE), lambda b, h, i: (b * nt + i, h)),
        out_shape=jax.ShapeDtypeStruct((bsz * seq, MLA_HEADS * MLA_V), F32),
        compiler_params=_params(("parallel", "parallel", "arbitrary")),
        name="mla_flash",
    )(q, k, v)


def _rwkv_prep_kernel(*refs, tm, has_vres):
    if has_vres:
        (p_ref, prev_ref, vf_ref, mu_ref, w0_ref, wup_ref, a0_ref, aup_ref, gup_ref, kk_ref, ka_ref,
         rk_ref, vd_ref, vu_ref, vb_ref, tri_ref, blk_ref, sel_ref, bd_ref,
         rh_o, ah_o, bh_o, kh_o, bt_o, kt_o, v_o, bon_o, g_o, gam_o) = refs
    else:
        (p_ref, prev_ref, mu_ref, w0_ref, wup_ref, a0_ref, aup_ref, gup_ref, kk_ref, ka_ref,
         rk_ref, tri_ref, blk_ref, sel_ref, bd_ref,
         rh_o, ah_o, bh_o, kh_o, bt_o, kt_o, v_o, bon_o, g_o, gam_o, vf_o) = refs
    i = pl.program_id(1)
    p = p_ref[...]
    first = jnp.where(i == 0, 0.0, prev_ref[7:8, :])
    rows = lax.broadcasted_iota(jnp.int32, p.shape, 0)
    prev = jnp.where(rows == 0, first, pltpu.roll(p, 1, axis=0))
    ps = p + (prev - p) * mu_ref[...]
    W = RWKV_WIDTH
    r, k, v = ps[:, :W], ps[:, W:2 * W], ps[:, 2 * W:3 * W]
    lora = ps[:, 3 * W:3 * W + DECAY_LORA + ICLR_LORA]
    g_d = ps[:, 3 * W + DECAY_LORA + ICLR_LORA:]
    w_log = -_softplus(-(w0_ref[...] + _dot(jnp.tanh(lora), wup_ref[...]))) - 0.5
    lw = -jnp.exp(w_log)
    iclr = _sigmoid(a0_ref[...] + _dot(lora, aup_ref[...]))
    g_o[...] = _dot(_sigmoid(g_d), gup_ref[...])
    if has_vres:
        mix = _sigmoid(vb_ref[...] + _dot(_dot(v, vd_ref[...]), vu_ref[...]))
        v = v + (vf_ref[...] - v) * mix
    bd = bd_ref[...]
    kk = k * kk_ref[...]
    kk = kk / jnp.maximum(jnp.sqrt(_dot_exact_rhs(kk * kk, bd)), 1e-12)
    k_h = k * (1.0 + (iclr - 1.0) * ka_ref[...])
    bon_o[...] = _dot_exact_rhs(r * k_h * rk_ref[...], bd) * v
    v_o[...] = v.astype(BF16)
    if not has_vres:
        vf_o[...] = v
    cum = _dot_exact_lhs(tri_ref[...], lw)
    tot = _dot_exact_lhs(blk_ref[...], lw)
    b = kk * iclr
    e_neg = jnp.exp(-cum)
    e_rem = jnp.exp(tot - cum)
    rh_o[...] = (r * jnp.exp(cum)).astype(BF16)
    ah_o[...] = (-kk * jnp.exp(cum - lw)).astype(BF16)
    bh_o[...] = (b * e_neg).astype(BF16)
    kh_o[...] = (k_h * e_neg).astype(BF16)
    bt_o[...] = (b * e_rem).astype(BF16)
    kt_o[...] = (k_h * e_rem).astype(BF16)
    gam_o[...] = jnp.exp(_dot_exact_lhs(sel_ref[...], lw))


def _rwkv_prep(p_rwkv, v_first, wts, consts, bsz, seq, tm):
    n = bsz * seq
    nt = seq // tm
    has_vres = v_first is not None
    row = lambda b, i: (b * nt + i, 0)
    prev_map = lambda b, i: (jnp.maximum((b * nt + i) * (tm // 8) - 1, 0), 0)
    wide = pl.BlockSpec((tm, RWKV_WIDTH), row)
    ins = [p_rwkv, p_rwkv]
    specs = [pl.BlockSpec((tm, RWKV_IN), row), pl.BlockSpec((8, RWKV_IN), prev_map)]
    if has_vres:
        ins.append(v_first)
        specs.append(wide)
    names = ["mu", "w0", "wup", "a0", "aup", "gup", "kk", "ka", "rk"]
    if has_vres:
        names += ["vd", "vu", "vb"]
    for nm in names:
        ins.append(wts[nm])
        specs.append(_full(wts[nm].shape))
    for c in consts:
        ins.append(c)
        specs.append(_full(c.shape))
    o = jax.ShapeDtypeStruct((n, RWKV_WIDTH), F32)
    ob = jax.ShapeDtypeStruct((n, RWKV_WIDTH), BF16)
    nch = tm // CHUNK
    extra = 0 if has_vres else 1
    return pl.pallas_call(
        functools.partial(_rwkv_prep_kernel, tm=tm, has_vres=has_vres),
        grid=(bsz, nt),
        in_specs=specs,
        out_specs=[wide] * 9 + [pl.BlockSpec((nch, RWKV_WIDTH), lambda b, i: (b * nt + i, 0))] + [wide] * extra,
        out_shape=[ob] * 7 + [o] * 2 + [jax.ShapeDtypeStruct((n // CHUNK, RWKV_WIDTH), F32)] + [o] * extra,
        compiler_params=_params(("parallel", "parallel")),
        name="rwkv_prep",
    )(*ins)


def _rwkv_scan_kernel(rh_ref, ah_ref, bh_ref, kh_ref, bt_ref, kt_ref, v_ref, bon_ref, g_ref, gam_ref,
                      lng_ref, lnb_ref, o_ref, s_ref, *, nch):
    c = pl.program_id(1)

    @pl.when(c == 0)
    def _():
        s_ref[...] = jnp.zeros_like(s_ref)

    C = CHUNK
    rr = lax.broadcasted_iota(jnp.int32, (C, C), 0)
    cc = lax.broadcasted_iota(jnp.int32, (C, C), 1)
    strict = rr > cc
    incl = rr >= cc
    eye = jnp.where(rr == cc, 1.0, 0.0).astype(F32)
    heads = range(RWKV_HEADS)
    sls = [slice(h * RWKV_HEAD, (h + 1) * RWKV_HEAD) for h in heads]
    pre = []
    for ci in range(nch):
        rows = slice(ci * C, (ci + 1) * C)
        ar = [jnp.concatenate([ah_ref[rows, sl], rh_ref[rows, sl]], axis=0) for sl in sls]
        bk = [jnp.concatenate([bh_ref[rows, sl], kh_ref[rows, sl]], axis=0) for sl in sls]
        gm = [_dot(ar[h], bk[h], NT) for h in heads]
        a_ab = [jnp.where(strict, g[:C, :C], 0.0) for g in gm]
        a_ak = [jnp.where(strict, g[:C, C:], 0.0) for g in gm]
        a_r = [jnp.concatenate([jnp.where(incl, g[C:, :C], 0.0), jnp.where(incl, g[C:, C:], 0.0)], axis=1)
               for g in gm]
        pre.append((ar, a_ab, a_ak, a_r))
    pw = [p[1] for p in pre]
    tinv = [[eye + a for a in p[1]] for p in pre]
    for _ in range(int(math.log2(C)) - 1):
        pw = [[_dot(p, p) for p in prow] for prow in pw]
        tinv = [[t + _dot(p, t) for p, t in zip(prow, trow)] for prow, trow in zip(pw, tinv)]
    s = [s_ref[h] for h in heads]
    for ci in range(nch):
        rows = slice(ci * C, (ci + 1) * C)
        ar, _, a_ak, a_r = pre[ci]
        gam = gam_ref[pl.ds((c * nch) % 8 + ci, 1), :]
        vv = [v_ref[rows, sl] for sl in sls]
        xs = [_dot(ar[h], s[h], NT) for h in heads]
        z = [xs[h][:C] + _dot(a_ak[h], vv[h]) for h in heads]
        u = [_dot(tinv[ci][h], z[h]) for h in heads]
        uv = [jnp.concatenate([u[h].astype(BF16), vv[h]], axis=0) for h in heads]
        o = [xs[h][C:] + _dot(a_r[h], uv[h]) for h in heads]
        btk = [jnp.concatenate([bt_ref[rows, sl], kt_ref[rows, sl]], axis=0) for sl in sls]
        s = [s[h] * gam[:, sls[h]] + _dot(uv[h], btk[h], TN) for h in heads]
        for h, sl in enumerate(sls):
            oc = o[h] - jnp.mean(o[h], axis=-1, keepdims=True)
            on = oc * lax.rsqrt(jnp.mean(oc * oc, axis=-1, keepdims=True) + RWKV_GN_EPS)
            y = on * lng_ref[:, sl] + lnb_ref[:, sl]
            o_ref[rows, sl] = (y + bon_ref[rows, sl]) * g_ref[rows, sl]
    for h in heads:
        s_ref[h] = s[h]


def _rwkv_scan(prep, ln_g, ln_b, bsz, seq, nch):
    n = bsz * seq
    tb = nch * CHUNK
    nc = seq // tb
    row = lambda b, c: (b * nc + c, 0)
    wide = pl.BlockSpec((tb, RWKV_WIDTH), row)
    return pl.pallas_call(
        functools.partial(_rwkv_scan_kernel, nch=nch),
        grid=(bsz, nc),
        in_specs=[wide] * 9 + [pl.BlockSpec((8, RWKV_WIDTH), lambda b, c: ((b * nc + c) * nch // 8, 0)),
                               _full((1, RWKV_WIDTH)), _full((1, RWKV_WIDTH))],
        out_specs=wide,
        out_shape=jax.ShapeDtypeStruct((n, RWKV_WIDTH), F32),
        scratch_shapes=[pltpu.VMEM((RWKV_HEADS, RWKV_HEAD, RWKV_HEAD), F32)],
        compiler_params=_params(("parallel", "arbitrary")),
        name="rwkv_scan",
    )(*prep, ln_g, ln_b)


def _rwkv_consts(tm):
    t = jnp.arange(tm)
    same = (t[:, None] // CHUNK) == (t[None, :] // CHUNK)
    tri = (same & (t[:, None] >= t[None, :])).astype(BF16)
    blk = same.astype(BF16)
    sel = ((jnp.arange(tm // CHUNK)[:, None]) == (t[None, :] // CHUNK)).astype(BF16)
    ch = jnp.arange(RWKV_WIDTH) // RWKV_HEAD
    bd = (ch[:, None] == ch[None, :]).astype(BF16)
    return tri, blk, sel, bd


def _prep_s5(a_re, a_im, log_dt, b_re, b_im, c_re, c_im, d, bsz):
    dt = jnp.exp(log_dt.astype(F32))[:, None]
    lam_re = jnp.minimum(a_re.astype(F32), -1e-4)
    lam_im = a_im.astype(F32)
    mag = jnp.exp(dt * lam_re)
    ab_re, ab_im = mag * jnp.cos(dt * lam_im), mag * jnp.sin(dt * lam_im)
    den = lam_re * lam_re + lam_im * lam_im
    f_re = ((ab_re - 1.0) * lam_re + ab_im * lam_im) / den
    f_im = (ab_im * lam_re - (ab_re - 1.0) * lam_im) / den
    bb_re = f_re[..., None] * b_re - f_im[..., None] * b_im
    bb_im = f_re[..., None] * b_im + f_im[..., None] * b_re
    eye = jnp.eye(S5_GROUPS, dtype=F32)

    def expand_b(bb):
        return jnp.einsum('gpc,gh->gchp', bb, eye).reshape(S5_WIDTH, S5_LANES)

    def expand_c(cm):
        return jnp.einsum('gcp,gh->gphc', cm.astype(F32), eye).reshape(S5_LANES, S5_WIDTH)

    tile = lambda a: jnp.broadcast_to(a.reshape(1, S5_LANES), (bsz, S5_LANES))
    return (expand_b(bb_re).astype(BF16), expand_b(bb_im).astype(BF16), tile(ab_re), tile(ab_im),
            expand_c(c_re).astype(BF16), expand_c(c_im).astype(BF16), d.astype(F32).reshape(1, S5_WIDTH))


def _s5_kernel(u_ref, bre_ref, bim_ref, are_ref, aim_ref, cre_ref, cim_ref, d_ref, wglu_ref, bglu_ref,
               o_ref, xre_s, xim_s, st_re, st_im, *, ts, bsz):
    @pl.when(pl.program_id(0) == 0)
    def _():
        st_re[...] = jnp.zeros_like(st_re)
        st_im[...] = jnp.zeros_like(st_im)

    u = u_ref[...]
    ub = u.astype(BF16)
    xre_s[...] = _dg(ub, bre_ref[...])
    xim_s[...] = _dg(ub, bim_ref[...])

    def step(t, carry):
        x_re, x_im = carry
        a_re = are_ref[...]
        a_im = aim_ref[...]
        r0 = pl.multiple_of(t * bsz, bsz)
        n_re = a_re * x_re - a_im * x_im + xre_s[pl.ds(r0, bsz), :]
        n_im = a_re * x_im + a_im * x_re + xim_s[pl.ds(r0, bsz), :]
        xre_s[pl.ds(r0, bsz), :] = n_re
        xim_s[pl.ds(r0, bsz), :] = n_im
        return n_re, n_im

    x_re, x_im = lax.fori_loop(0, ts, step, (st_re[...], st_im[...]))
    st_re[...] = x_re
    st_im[...] = x_im
    y = _dot(xre_s[...], cre_ref[...]) - _dot(xim_s[...], cim_ref[...]) + d_ref[...] * u
    y = _gelu_tanh(y)
    o_ref[...] = y * _sigmoid(_dot(y, wglu_ref[...]) + bglu_ref[...])


def _s5(u_tm, s5w, w_glu, b_glu, bsz, seq, ts):
    rows = ts * bsz
    bre, bim, are, aim, cre, cim, d = s5w
    tile = pl.BlockSpec((rows, S5_WIDTH), lambda i: (i, 0))
    return pl.pallas_call(
        functools.partial(_s5_kernel, ts=ts, bsz=bsz),
        grid=(seq // ts,),
        in_specs=[tile, _full(bre.shape), _full(bim.shape), _full(are.shape), _full(aim.shape),
                  _full(cre.shape), _full(cim.shape), _full(d.shape), _full(w_glu.shape), _full(b_glu.shape)],
        out_specs=tile,
        out_shape=jax.ShapeDtypeStruct((seq * bsz, S5_WIDTH), F32),
        scratch_shapes=[pltpu.VMEM((rows, S5_LANES), F32), pltpu.VMEM((rows, S5_LANES), F32),
                        pltpu.VMEM((bsz, S5_LANES), F32), pltpu.VMEM((bsz, S5_LANES), F32)],
        compiler_params=_params(("arbitrary",)),
        name="s5_scan",
    )(u_tm, bre, bim, are, aim, cre, cim, d, w_glu, b_glu)


def _merge_kernel(*refs, with_router):
    if with_router:
        (x_ref, ym_ref, yr_ref, ys_ref, gt_ref, wb_ref, wo_ref, g_ref, b_ref, rt_ref,
         o_ref, ob_ref, lg_ref) = refs
    else:
        x_ref, ym_ref, yr_ref, ys_ref, gt_ref, wb_ref, wo_ref, g_ref, b_ref, o_ref = refs
    mixed = None
    for n, y_ref in enumerate((ym_ref, yr_ref, ys_ref)):
        t = gt_ref[:, n * D_MODEL:(n + 1) * D_MODEL] * _dot(y_ref[...], wb_ref[n])
        mixed = t if mixed is None else mixed + t
    z = DEEPNORM_ALPHA * x_ref[...] + _dot(mixed, wo_ref[...])
    out = _layer_norm(z, g_ref[...], b_ref[...])
    o_ref[...] = out
    if with_router:
        ob_ref[...] = out.astype(BF16)
        lg_ref[...] = _dot3(out, rt_ref[...])


def _merge(x2, y_mla, y_rwkv, y_s5_tm, gates, wb, wo, g, b, router, bsz, seq, tm):
    n = bsz * seq
    nt = seq // tm
    row = lambda bb, i: (bb * nt + i, 0)
    with_router = router is not None
    ins = [x2, y_mla, y_rwkv, y_s5_tm, gates, wb, wo, g, b]
    specs = [pl.BlockSpec((tm, D_MODEL), row),
             pl.BlockSpec((tm, 512), row), pl.BlockSpec((tm, 512), row),
             pl.BlockSpec((tm, S5_WIDTH), lambda bb, i: (i, bb)),
             pl.BlockSpec((tm, N_BRANCH * D_MODEL), row),
             _full(wb.shape), _full(wo.shape), _full(g.shape), _full(b.shape)]
    out_specs = [pl.BlockSpec((tm, D_MODEL), row)]
    out_shape = [jax.ShapeDtypeStruct((n, D_MODEL), F32)]
    if with_router:
        ins.append(router)
        specs.append(_full(router.shape))
        out_specs += [pl.BlockSpec((tm, D_MODEL), row), pl.BlockSpec((tm, LANE), row)]
        out_shape += [jax.ShapeDtypeStruct((n, D_MODEL), BF16), jax.ShapeDtypeStruct((n, LANE), F32)]
    res = pl.pallas_call(
        functools.partial(_merge_kernel, with_router=with_router),
        grid=(bsz, nt),
        in_specs=specs,
        out_specs=out_specs,
        out_shape=out_shape,
        compiler_params=_params(("parallel", "parallel")),
        name="merge_ln1",
    )(*ins)
    return res if with_router else (res[0], None, None)


def _ffn_kernel(x_ref, wg_ref, wu_ref, wd_ref, g_ref, b_ref, o_ref, acc_ref):
    j = pl.program_id(1)
    x = x_ref[...]
    xb = x.astype(BF16)
    h = _silu(_dg(xb, wg_ref[...])) * _dg(xb, wu_ref[...])
    part = _dot(h, wd_ref[...])

    @pl.when(j == 0)
    def _():
        acc_ref[...] = part

    @pl.when(j > 0)
    def _():
        acc_ref[...] += part

    @pl.when(j == pl.num_programs(1) - 1)
    def _():
        o_ref[...] = _layer_norm(DEEPNORM_ALPHA * x + acc_ref[...], g_ref[...], b_ref[...])


def _ffn(x2, wg, wu, wd, g, b, tm, tf):
    n = x2.shape[0]
    return pl.pallas_call(
        _ffn_kernel,
        grid=(n // tm, FFN_DIM // tf),
        in_specs=[pl.BlockSpec((tm, D_MODEL), lambda i, j: (i, 0)),
                  pl.BlockSpec((D_MODEL, tf), lambda i, j: (0, j)),
                  pl.BlockSpec((D_MODEL, tf), lambda i, j: (0, j)),
                  pl.BlockSpec((tf, D_MODEL), lambda i, j: (j, 0)),
                  _full(g.shape), _full(b.shape)],
        out_specs=pl.BlockSpec((tm, D_MODEL), lambda i, j: (i, 0)),
        out_shape=jax.ShapeDtypeStruct((n, D_MODEL), F32),
        scratch_shapes=[pltpu.VMEM((tm, D_MODEL), F32)],
        compiler_params=_params(("parallel", "arbitrary")),
        name="ffn_ln2",
    )(x2, wg, wu, wd, g, b)


def _moe_kernel(be_ref, nu_ref, xs_ref, sw_ref, wg_ref, wu_ref, wd_ref, o_ref, acc_ref):
    i = pl.program_id(0)
    j = pl.program_id(1)

    @pl.when(i < nu_ref[0])
    def _():
        xb = xs_ref[...]
        h = _silu(_dg(xb, wg_ref[0])) * _dg(xb, wu_ref[0])
        part = _dot(h, wd_ref[0])

        @pl.when(j == 0)
        def _():
            acc_ref[...] = part

        @pl.when(j > 0)
        def _():
            acc_ref[...] += part

        @pl.when(j == pl.num_programs(1) - 1)
        def _():
            o_ref[...] = acc_ref[...] * sw_ref[...]

    @pl.when((i >= nu_ref[0]) & (j == pl.num_programs(1) - 1))
    def _():
        o_ref[...] = jnp.zeros_like(o_ref)


def _moe_experts(xs, slot_w, block_e, n_used, wg, wu, wd, tf):
    n_slots = xs.shape[0]
    n_blocks = n_slots // MOE_BLOCK
    nf = EXPERT_DIM // tf

    def blk(i, nu):
        return jnp.minimum(i, nu[0] - 1)

    def fcol(i, j, nu):
        return jnp.where(i < nu[0], j, nf - 1)

    grid_spec = pltpu.PrefetchScalarGridSpec(
        num_scalar_prefetch=2,
        grid=(n_blocks, nf),
        in_specs=[pl.BlockSpec((MOE_BLOCK, D_MODEL), lambda i, j, be, nu: (blk(i, nu), 0)),
                  pl.BlockSpec((MOE_BLOCK, 1), lambda i, j, be, nu: (blk(i, nu), 0)),
                  pl.BlockSpec((1, D_MODEL, tf), lambda i, j, be, nu: (be[blk(i, nu)], 0, fcol(i, j, nu))),
                  pl.BlockSpec((1, D_MODEL, tf), lambda i, j, be, nu: (be[blk(i, nu)], 0, fcol(i, j, nu))),
                  pl.BlockSpec((1, tf, D_MODEL), lambda i, j, be, nu: (be[blk(i, nu)], fcol(i, j, nu), 0))],
        out_specs=pl.BlockSpec((MOE_BLOCK, D_MODEL), lambda i, j, be, nu: (i, 0)),
        scratch_shapes=[pltpu.VMEM((MOE_BLOCK, D_MODEL), F32)],
    )
    return pl.pallas_call(
        _moe_kernel,
        grid_spec=grid_spec,
        out_shape=jax.ShapeDtypeStruct((n_slots, D_MODEL), F32),
        compiler_params=_params(("arbitrary", "arbitrary")),
        name="moe_experts",
    )(block_e, n_used, xs, slot_w, wg, wu, wd)


def _combine_kernel(x_ref, y0_ref, y1_ref, g_ref, b_ref, o_ref):
    z = DEEPNORM_ALPHA * x_ref[...] + (y0_ref[...] + y1_ref[...])
    o_ref[...] = _layer_norm(z, g_ref[...], b_ref[...])


def _combine(x2, y0, y1, g, b, tm):
    n = x2.shape[0]
    tile = pl.BlockSpec((tm, D_MODEL), lambda i: (i, 0))
    return pl.pallas_call(
        _combine_kernel,
        grid=(n // tm,),
        in_specs=[tile, tile, tile, _full(g.shape), _full(b.shape)],
        out_specs=tile,
        out_shape=jax.ShapeDtypeStruct((n, D_MODEL), F32),
        compiler_params=_params(("parallel",)),
        name="moe_combine_ln2",
    )(x2, y0, y1, g, b)


def _moe(h2, h2b, logits, wg, wu, wd, g, b, tm, tf):
    n_tok = h2.shape[0]
    n_pair = n_tok * TOP_K
    top_logit, top_e = lax.top_k(logits[:, :N_EXPERTS], TOP_K)
    top_w = jax.nn.softmax(top_logit, axis=-1)
    pair_e = top_e.reshape(-1).astype(jnp.int32)
    onehot = (pair_e[:, None] == jnp.arange(N_EXPERTS, dtype=jnp.int32)[None, :]).astype(jnp.int32)
    csum = jnp.cumsum(onehot, axis=0)
    counts = csum[-1]
    start = jnp.cumsum(counts) - counts
    padded = (counts + MOE_BLOCK - 1) // MOE_BLOCK * MOE_BLOCK
    pad_end = jnp.cumsum(padded)
    pad_start = pad_end - padded
    pair_slot = jnp.sum(onehot * (pad_start[None, :] + csum - 1), axis=1)
    n_blocks = n_pair // MOE_BLOCK + N_EXPERTS
    n_slots = n_blocks * MOE_BLOCK
    block_e = jnp.minimum(jnp.searchsorted(pad_end, jnp.arange(n_blocks, dtype=jnp.int32) * MOE_BLOCK,
                                           side='right'), N_EXPERTS - 1).astype(jnp.int32)
    n_used = (pad_end[-1] // MOE_BLOCK).astype(jnp.int32).reshape(1)
    pair_tok = jnp.arange(n_pair, dtype=jnp.int32) // TOP_K
    _, tok_sorted, w_sorted = lax.sort((pair_e, pair_tok, top_w.reshape(-1)), num_keys=1, is_stable=True)
    tail = n_slots - n_pair
    tok_pad = jnp.concatenate([tok_sorted, jnp.zeros((tail,), jnp.int32)])
    w_pad = jnp.concatenate([w_sorted, jnp.zeros((tail,), F32)])
    slot = jnp.arange(n_slots, dtype=jnp.int32)
    slot_e = jnp.repeat(block_e, MOE_BLOCK)
    valid = slot < jnp.repeat((pad_start + counts)[block_e], MOE_BLOCK)
    shift = pad_start - start
    slot_tok = jnp.zeros((n_slots,), jnp.int32)
    slot_w = jnp.zeros((n_slots,), F32)
    for e in range(N_EXPERTS):
        here = valid & (slot_e == e)
        slot_tok = jnp.where(here, jnp.roll(tok_pad, shift[e]), slot_tok)
        slot_w = jnp.where(here, jnp.roll(w_pad, shift[e]), slot_w)
    xs = jnp.take(h2b, slot_tok, axis=0, mode="clip")
    y = _moe_experts(xs, slot_w[:, None], block_e, n_used, wg, wu, wd, tf)
    ps = pair_slot.reshape(n_tok, TOP_K)
    y0 = jnp.take(y, ps[:, 0], axis=0, mode="clip")
    y1 = jnp.take(y, ps[:, 1], axis=0, mode="clip")
    return _combine(h2, y0, y1, g, b, tm)


def kernel(x, positions, w_in, mla_q_norm, mla_w_uq, mla_kv_norm, mla_w_ukv,
           rwkv_mu, rwkv_w0, rwkv_w_up, rwkv_a0, rwkv_a_up, rwkv_g_up, rwkv_k_k, rwkv_k_a,
           rwkv_r_k, rwkv_ln_g, rwkv_ln_b, rwkv_vres_down, rwkv_vres_up, rwkv_vres_b,
           s5_a_re, s5_a_im, s5_log_dt, s5_b_re, s5_b_im, s5_c_re, s5_c_im, s5_d, s5_w_glu, s5_b_glu,
           w_branch, w_out, ln1_g, ln1_b, ln2_g, ln2_b,
           ffn_w_gate, ffn_w_up, ffn_w_down, moe_router, moe_w_gate, moe_w_up, moe_w_down):
    bsz, seq, _ = x.shape
    n = bsz * seq
    depth = w_in.shape[0]
    tm = min(512, seq)
    tq = min(FLASH_TQ, seq)
    ts = min(64, seq)
    row = lambda a: a.reshape(1, -1)

    cos, sin = _rope_tables(positions, tm)
    w_in_p = _prep_w_in(w_in)
    consts = _rwkv_consts(tm)
    zpad = jnp.zeros((DECAY_LORA, RWKV_WIDTH), F32)

    h = x.reshape(n, D_MODEL)
    v_first = None
    for l in range(depth):
        p_mla, p_rwkv, u_s5, gates = _in_proj(h, w_in_p[l], bsz, seq, tm)

        wq, wqr, wk, wv = _prep_mla_weights(mla_w_uq[l], mla_w_ukv[l])
        q, k, v = _mla_prep(p_mla, cos, sin, row(mla_q_norm[l]), row(mla_kv_norm[l]),
                            wq, wqr, wk, wv, bsz, seq, tm)
        y_mla = _flash(q, k, v, bsz, seq, tq, min(FLASH_TD, tq))

        wts = {
            "mu": row(rwkv_mu[l]), "w0": row(rwkv_w0[l]),
            "wup": jnp.concatenate([rwkv_w_up[l], zpad], 0).astype(BF16),
            "a0": row(rwkv_a0[l]),
            "aup": jnp.concatenate([zpad, rwkv_a_up[l]], 0).astype(BF16),
            "gup": rwkv_g_up[l].astype(BF16),
            "kk": row(rwkv_k_k[l]), "ka": row(rwkv_k_a[l]), "rk": row(rwkv_r_k[l]),
        }
        if l > 0:
            wts["vd"] = rwkv_vres_down[l - 1].astype(BF16)
            wts["vu"] = rwkv_vres_up[l - 1].astype(BF16)
            wts["vb"] = row(rwkv_vres_b[l - 1])
        prep = _rwkv_prep(p_rwkv, v_first, wts, consts, bsz, seq, tm)
        if l == 0:
            v_first = prep[10]
        prep = prep[:10]
        y_rwkv = _rwkv_scan(prep, row(rwkv_ln_g[l]), row(rwkv_ln_b[l]), bsz, seq, RWKV_NCH)

        s5w = _prep_s5(s5_a_re[l], s5_a_im[l], s5_log_dt[l], s5_b_re[l], s5_b_im[l],
                       s5_c_re[l], s5_c_im[l], s5_d[l], bsz)
        y_s5 = _s5(u_s5.reshape(seq * bsz, S5_WIDTH), s5w, s5_w_glu[l].astype(BF16), row(s5_b_glu[l]),
                   bsz, seq, ts)

        moe_layer = l % 2 == 1
        router = None
        if moe_layer:
            router = jnp.pad(moe_router[l // 2], ((0, 0), (0, LANE - N_EXPERTS)))
        h, hb, logits = _merge(h, y_mla, y_rwkv, y_s5.reshape(seq, bsz * S5_WIDTH), gates,
                               w_branch[l].astype(BF16), w_out[l].astype(BF16), row(ln1_g[l]), row(ln1_b[l]),
                               router, bsz, seq, tm)
        if moe_layer:
            h = _moe(h, hb, logits, moe_w_gate[l // 2].astype(BF16), moe_w_up[l // 2].astype(BF16),
                     moe_w_down[l // 2].astype(BF16), row(ln2_g[l]), row(ln2_b[l]), tm, tf=896)
        else:
            h = _ffn(h, ffn_w_gate[l // 2].astype(BF16), ffn_w_up[l // 2].astype(BF16),
                     ffn_w_down[l // 2].astype(BF16), row(ln2_g[l]), row(ln2_b[l]), tm, tf=1408)
    return h.reshape(bsz, seq, D_MODEL)
```

```python
import functools
import math

import jax
import jax.numpy as jnp
from jax import lax
from jax.experimental import pallas as pl
from jax.experimental.pallas import tpu as pltpu

F32 = jnp.float32
BF16 = jnp.bfloat16

D_MODEL = 1024
MLA_HEADS = 8
MLA_NOPE = 64
MLA_ROPE = 32
MLA_V = 64
Q_LORA = 384
KV_LORA = 256
ROPE_THETA = 10000.0
NEG_INF = -1e30
RWKV_HEADS = 8
RWKV_HEAD = 64
RWKV_WIDTH = 512
DECAY_LORA = 64
ICLR_LORA = 64
GATE_LORA = 128
RWKV_GN_EPS = 64e-5
RWKV_IN = 3 * RWKV_WIDTH + DECAY_LORA + ICLR_LORA + GATE_LORA
S5_GROUP = 16
S5_GROUPS = 32
S5_WIDTH = 512
S5_STATE = 64
S5_LANES = S5_GROUPS * S5_STATE
N_BRANCH = 3
FFN_DIM = 2816
N_EXPERTS = 8
TOP_K = 2
EXPERT_DIM = 3584
MOE_BLOCK = 512
DEPTH = 4
DEEPNORM_ALPHA = (2 * DEPTH) ** 0.25

LANE = 128
MLA_PAD = 896
CHUNK = 64
FLASH_TQ = 1024
FLASH_TD = 512
RWKV_NCH = 4
VMEM_LIMIT = 56 * 1024 * 1024

NN = ((1,), (0,))
NT = ((1,), (1,))
TN = ((0,), (0,))


def _dg(a, b, dims=NN):
    return lax.dot_general(a, b, (dims, ((), ())), preferred_element_type=F32)


def _dot(a, b, dims=NN):
    return _dg(a.astype(BF16), b.astype(BF16), dims)


def _split2(a):
    hi = a.astype(BF16)
    lo = (a - hi.astype(F32)).astype(BF16)
    return hi, lo


def _split3(a):
    hi = a.astype(BF16)
    r1 = a - hi.astype(F32)
    mid = r1.astype(BF16)
    lo = (r1 - mid.astype(F32)).astype(BF16)
    return hi, mid, lo


def _dot3(a, b, dims=NN):
    ah, al = _split2(a)
    bh, bl = _split2(b)
    return _dg(ah, bh, dims) + (_dg(ah, bl, dims) + _dg(al, bh, dims))


def _dot_exact_lhs(c_bf16, x):
    hi, mid, lo = _split3(x)
    return _dg(c_bf16, hi) + (_dg(c_bf16, mid) + _dg(c_bf16, lo))


def _dot_exact_rhs(x, c_bf16):
    hi, mid, lo = _split3(x)
    return _dg(hi, c_bf16) + (_dg(mid, c_bf16) + _dg(lo, c_bf16))


def _layer_norm(z, g, b, eps=1e-5):
    zc = z - jnp.mean(z, axis=-1, keepdims=True)
    var = jnp.mean(zc * zc, axis=-1, keepdims=True)
    return zc * lax.rsqrt(var + eps) * g + b


def _rms(x, g, eps=1e-6):
    return x * lax.rsqrt(jnp.mean(x * x, axis=-1, keepdims=True) + eps) * g


def _sigmoid(x):
    return 1.0 / (1.0 + jnp.exp(-x))


def _silu(x):
    return x * _sigmoid(x)


def _softplus(z):
    return jnp.maximum(z, 0.0) + jnp.log(1.0 + jnp.exp(-jnp.abs(z)))


def _gelu_tanh(x):
    c = math.sqrt(2.0 / math.pi)
    return x * (0.5 * (1.0 + jnp.tanh(c * (x + 0.044715 * (x * x * x)))))


def _params(sem, vmem=VMEM_LIMIT):
    return pltpu.CompilerParams(dimension_semantics=sem, vmem_limit_bytes=vmem)


def _full(shape):
    nd = len(shape)
    return pl.BlockSpec(shape, lambda *_: (0,) * nd, pipeline_mode=pl.Buffered(1))


def _rope_kernel(pos_ref, freq_ref, c_ref, s_ref):
    ang = pos_ref[...].astype(F32) * freq_ref[...]
    lane = lax.broadcasted_iota(jnp.int32, ang.shape, 1)
    c_ref[...] = jnp.where(lane < MLA_NOPE + MLA_ROPE, jnp.cos(ang), 0.0)
    s_ref[...] = jnp.sin(ang)


def _rope_tables(positions, tm):
    n = positions.size
    half = MLA_ROPE // 2
    inv_freq = ROPE_THETA ** (-jnp.arange(0, MLA_ROPE, 2, dtype=F32) / MLA_ROPE)
    freq = jnp.concatenate([jnp.zeros((MLA_NOPE,), F32), inv_freq, inv_freq,
                            jnp.zeros((LANE - MLA_NOPE - 2 * half,), F32)])[None, :]
    out = jax.ShapeDtypeStruct((n, LANE), F32)
    return pl.pallas_call(
        _rope_kernel,
        grid=(n // tm,),
        in_specs=[pl.BlockSpec((tm, 1), lambda i: (i, 0)), _full((1, LANE))],
        out_specs=[pl.BlockSpec((tm, LANE), lambda i: (i, 0))] * 2,
        out_shape=[out, out],
        compiler_params=_params(("parallel",)),
        name="rope_tables",
    )(positions.reshape(n, 1), freq)


IN_SPLITS = (MLA_PAD, RWKV_IN, S5_WIDTH, N_BRANCH * D_MODEL)
IN_PAD = sum(IN_SPLITS)


def _prep_w_in(w):
    cq = w[..., :Q_LORA]
    ckv = w[..., Q_LORA:Q_LORA + KV_LORA]
    kpe = w[..., Q_LORA + KV_LORA:Q_LORA + KV_LORA + MLA_ROPE]
    half = MLA_ROPE // 2
    k1, k2 = kpe[..., :half], kpe[..., half:]
    z64 = jnp.zeros(w.shape[:-1] + (MLA_NOPE,), w.dtype)
    z32 = jnp.zeros(w.shape[:-1] + (LANE - MLA_NOPE - MLA_ROPE,), w.dtype)
    rest = w[..., Q_LORA + KV_LORA + MLA_ROPE:]
    return jnp.concatenate([cq, ckv, z64, k1, k2, z32, z64, -k2, k1, z32, rest], axis=-1).astype(BF16)


def _in_proj_kernel(x_ref, w_ref, mla_ref, rwkv_ref, s5_ref, gate_ref):
    x = x_ref[...].astype(BF16)
    o0 = 0
    o1 = o0 + MLA_PAD
    o2 = o1 + RWKV_IN
    o3 = o2 + S5_WIDTH
    mla_ref[...] = _dg(x, w_ref[:, o0:o1])
    rwkv_ref[...] = _dg(x, w_ref[:, o1:o2])
    s5_ref[...] = _dg(x, w_ref[:, o2:o3])
    for n in range(N_BRANCH):
        c0 = o3 + n * D_MODEL
        gate_ref[:, n * D_MODEL:(n + 1) * D_MODEL] = _sigmoid(_dg(x, w_ref[:, c0:c0 + D_MODEL])).astype(BF16)


def _in_proj(x2, w, bsz, seq, tm):
    n = bsz * seq
    nt = seq // tm
    row = lambda b, i: (b * nt + i, 0)
    return pl.pallas_call(
        _in_proj_kernel,
        grid=(bsz, nt),
        in_specs=[pl.BlockSpec((tm, D_MODEL), row), _full((D_MODEL, IN_PAD))],
        out_specs=[pl.BlockSpec((tm, MLA_PAD), row),
                   pl.BlockSpec((tm, RWKV_IN), row),
                   pl.BlockSpec((tm, S5_WIDTH), lambda b, i: (i, b)),
                   pl.BlockSpec((tm, N_BRANCH * D_MODEL), row)],
        out_shape=[jax.ShapeDtypeStruct((n, MLA_PAD), F32),
                   jax.ShapeDtypeStruct((n, RWKV_IN), F32),
                   jax.ShapeDtypeStruct((seq, bsz * S5_WIDTH), F32),
                   jax.ShapeDtypeStruct((n, N_BRANCH * D_MODEL), BF16)],
        compiler_params=_params(("parallel", "parallel")),
        name="in_proj",
    )(x2, w)


def _prep_mla_weights(w_uq, w_ukv):
    half = MLA_ROPE // 2
    wq = w_uq.reshape(Q_LORA, MLA_HEADS, MLA_NOPE + MLA_ROPE)
    nope, pe1, pe2 = wq[..., :MLA_NOPE], wq[..., MLA_NOPE:MLA_NOPE + half], wq[..., MLA_NOPE + half:]
    z32 = jnp.zeros((Q_LORA, MLA_HEADS, LANE - MLA_NOPE - MLA_ROPE), w_uq.dtype)
    z64 = jnp.zeros((Q_LORA, MLA_HEADS, MLA_NOPE), w_uq.dtype)
    wq_main = jnp.concatenate([nope, pe1, pe2, z32], -1).reshape(Q_LORA, MLA_HEADS * LANE)
    wq_rot = jnp.concatenate([z64, -pe2, pe1, z32], -1).reshape(Q_LORA, MLA_HEADS * LANE)
    wkv = w_ukv.reshape(KV_LORA, MLA_HEADS, MLA_NOPE + MLA_V)
    k_nope, v = wkv[..., :MLA_NOPE], wkv[..., MLA_NOPE:]
    zk = jnp.zeros((KV_LORA, MLA_HEADS, LANE - MLA_NOPE), w_ukv.dtype)
    wk = jnp.concatenate([k_nope, zk], -1).reshape(KV_LORA, MLA_HEADS * LANE)
    zv = jnp.zeros_like(v)
    even = (jnp.arange(MLA_HEADS) % 2 == 0)[None, :, None]
    wv = jnp.concatenate([jnp.where(even, v, zv), jnp.where(even, zv, v)], -1).reshape(KV_LORA, MLA_HEADS * LANE)
    return wq_main.astype(BF16), wq_rot.astype(BF16), wk.astype(BF16), wv.astype(BF16)


def _mla_prep_kernel(p_ref, c_ref, s_ref, qn_ref, kvn_ref, wq_ref, wqr_ref, wk_ref, wv_ref,
                     q_out, k_out, v_out):
    cos = c_ref[...]
    sin = s_ref[...]
    qn = _rms(p_ref[:, :Q_LORA], qn_ref[...]).astype(BF16)
    kvn = _rms(p_ref[:, Q_LORA:Q_LORA + KV_LORA], kvn_ref[...]).astype(BF16)
    o = Q_LORA + KV_LORA
    k_rope = p_ref[:, o:o + LANE] * cos + p_ref[:, o + LANE:o + 2 * LANE] * sin
    scale = (MLA_NOPE + MLA_ROPE) ** -0.5 * math.log2(math.e)
    lane = lax.broadcasted_iota(jnp.int32, (1, LANE), 1)
    for h in range(MLA_HEADS):
        cols = slice(h * LANE, (h + 1) * LANE)
        q = _dg(qn, wq_ref[:, cols]) * cos + _dg(qn, wqr_ref[:, cols]) * sin
        q_out[0, h] = (q * scale).astype(BF16)
        k_out[0, h] = (_dg(kvn, wk_ref[:, cols]) + k_rope).astype(BF16)
        ones = jnp.where(lane == (1 - h % 2) * MLA_V, 1.0, 0.0)
        v_out[0, h] = (_dg(kvn, wv_ref[:, cols]) + ones).astype(BF16)


def _mla_prep(p_mla, cos, sin, q_norm, kv_norm, wq, wqr, wk, wv, bsz, seq, tm):
    nt = seq // tm
    row = lambda b, i: (b * nt + i, 0)
    hw = MLA_HEADS * LANE
    out = jax.ShapeDtypeStruct((bsz, MLA_HEADS, seq, LANE), BF16)
    ospec = pl.BlockSpec((1, MLA_HEADS, tm, LANE), lambda b, i: (b, 0, i, 0))
    return pl.pallas_call(
        _mla_prep_kernel,
        grid=(bsz, nt),
        in_specs=[pl.BlockSpec((tm, MLA_PAD), row),
                  pl.BlockSpec((tm, LANE), row), pl.BlockSpec((tm, LANE), row),
                  _full((1, Q_LORA)), _full((1, KV_LORA)),
                  _full((Q_LORA, hw)), _full((Q_LORA, hw)), _full((KV_LORA, hw)), _full((KV_LORA, hw))],
        out_specs=[ospec, ospec, ospec],
        out_shape=[out, out, out],
        compiler_params=_params(("parallel", "parallel")),
        name="mla_prep",
    )(p_mla, cos, sin, q_norm, kv_norm, wq, wqr, wk, wv)


def _flash_kernel(q_ref, k_ref, v_ref, o_ref, *, tq, td):
    i = pl.program_id(2)
    lane = lax.broadcasted_iota(jnp.int32, (tq, LANE), 1)
    qs = [q_ref[0, hh] for hh in range(2)]

    def scores(hh, start, size, r0=0):
        kj = k_ref[0, hh, pl.ds(start, size), :]
        vj = v_ref[0, hh, pl.ds(start, size), :]
        return _dg(qs[hh][r0:], kj, NT), vj

    def update(carry, s, vj):
        m, acc = carry
        m_new = jnp.maximum(m, jnp.max(s, axis=-1, keepdims=True))
        p = jnp.exp2(s - m_new)
        acc = jnp.exp2(m - m_new) * acc + _dg(p.astype(BF16), vj)
        return m_new, acc

    def body(j, carry):
        start = pl.multiple_of(j * tq, tq)
        return tuple(update(carry[hh], *scores(hh, start, tq)) for hh in range(2))

    init = (jnp.full((tq, 1), NEG_INF, F32), jnp.zeros((tq, LANE), F32))
    carry = lax.fori_loop(0, i, body, (init, init))
    for d in range(tq // td):
        r0 = d * td
        start = pl.multiple_of(i * tq + r0, td)
        causal = (lax.broadcasted_iota(jnp.int32, (tq - r0, td), 0)
                  >= lax.broadcasted_iota(jnp.int32, (tq - r0, td), 1))
        stepped = []
        for hh in range(2):
            m, acc = carry[hh]
            s, vj = scores(hh, start, td, r0)
            m_low, acc_low = update((m[r0:], acc[r0:]), jnp.where(causal, s, NEG_INF), vj)
            if r0:
                m_low = jnp.concatenate([m[:r0], m_low], axis=0)
                acc_low = jnp.concatenate([acc[:r0], acc_low], axis=0)
            stepped.append((m_low, acc_low))
        carry = tuple(stepped)
    out = None
    for hh in range(2):
        acc = carry[hh][1]
        ones_lane = (1 - hh) * MLA_V
        mine = (lane >= hh * MLA_V) & (lane < (hh + 1) * MLA_V)
        l = jnp.sum(jnp.where(lane == ones_lane, acc, 0.0), axis=-1, keepdims=True)
        o = jnp.where(mine, acc / l, 0.0)
        out = o if out is None else out + o
    o_ref[...] = out


def _flash(q, k, v, bsz, seq, tq, td):
    nt = seq // tq
    return pl.pallas_call(
        functools.partial(_flash_kernel, tq=tq, td=td),
        grid=(bsz, MLA_HEADS // 2, nt),
        in_specs=[pl.BlockSpec((1, 2, tq, LANE), lambda b, h, i: (b, h, i, 0)),
                  pl.BlockSpec((1, 2, seq, LANE), lambda b, h, i: (b, h, 0, 0)),
                  pl.BlockSpec((1, 2, seq, LANE), lambda b, h, i: (b, h, 0, 0))],
        out_specs=pl.BlockSpec((tq, LANE), lambda b, h, i: (b * nt + i, h)),
        out_shape=jax.ShapeDtypeStruct((bsz * seq, MLA_HEADS * MLA_V), F32),
        compiler_params=_params(("parallel", "parallel", "arbitrary")),
        name="mla_flash",
    )(q, k, v)


def _rwkv_prep_kernel(*refs, tm, has_vres):
    if has_vres:
        (p_ref, prev_ref, vf_ref, mu_ref, w0_ref, wup_ref, a0_ref, aup_ref, gup_ref, kk_ref, ka_ref,
         rk_ref, vd_ref, vu_ref, vb_ref, tri_ref, blk_ref, sel_ref, bd_ref,
         rh_o, ah_o, bh_o, kh_o, bt_o, kt_o, v_o, bon_o, g_o, gam_o) = refs
    else:
        (p_ref, prev_ref, mu_ref, w0_ref, wup_ref, a0_ref, aup_ref, gup_ref, kk_ref, ka_ref,
         rk_ref, tri_ref, blk_ref, sel_ref, bd_ref,
         rh_o, ah_o, bh_o, kh_o, bt_o, kt_o, v_o, bon_o, g_o, gam_o, vf_o) = refs
    i = pl.program_id(1)
    p = p_ref[...]
    first = jnp.where(i == 0, 0.0, prev_ref[7:8, :])
    rows = lax.broadcasted_iota(jnp.int32, p.shape, 0)
    prev = jnp.where(rows == 0, first, pltpu.roll(p, 1, axis=0))
    ps = p + (prev - p) * mu_ref[...]
    W = RWKV_WIDTH
    r, k, v = ps[:, :W], ps[:, W:2 * W], ps[:, 2 * W:3 * W]
    lora = ps[:, 3 * W:3 * W + DECAY_LORA + ICLR_LORA]
    g_d = ps[:, 3 * W + DECAY_LORA + ICLR_LORA:]
    w_log = -_softplus(-(w0_ref[...] + _dot(jnp.tanh(lora), wup_ref[...]))) - 0.5
    lw = -jnp.exp(w_log)
    iclr = _sigmoid(a0_ref[...] + _dot(lora, aup_ref[...]))
    g_o[...] = _dot(_sigmoid(g_d), gup_ref[...])
    if has_vres:
        mix = _sigmoid(vb_ref[...] + _dot(_dot(v, vd_ref[...]), vu_ref[...]))
        v = v + (vf_ref[...] - v) * mix
    bd = bd_ref[...]
    kk = k * kk_ref[...]
    kk = kk / jnp.maximum(jnp.sqrt(_dot_exact_rhs(kk * kk, bd)), 1e-12)
    k_h = k * (1.0 + (iclr - 1.0) * ka_ref[...])
    bon_o[...] = _dot_exact_rhs(r * k_h * rk_ref[...], bd) * v
    v_o[...] = v.astype(BF16)
    if not has_vres:
        vf_o[...] = v
    cum = _dot_exact_lhs(tri_ref[...], lw)
    tot = _dot_exact_lhs(blk_ref[...], lw)
    b = kk * iclr
    e_neg = jnp.exp(-cum)
    e_rem = jnp.exp(tot - cum)
    rh_o[...] = (r * jnp.exp(cum)).astype(BF16)
    ah_o[...] = (-kk * jnp.exp(cum - lw)).astype(BF16)
    bh_o[...] = (b * e_neg).astype(BF16)
    kh_o[...] = (k_h * e_neg).astype(BF16)
    bt_o[...] = (b * e_rem).astype(BF16)
    kt_o[...] = (k_h * e_rem).astype(BF16)
    gam_o[...] = jnp.exp(_dot_exact_lhs(sel_ref[...], lw))


def _rwkv_prep(p_rwkv, v_first, wts, consts, bsz, seq, tm):
    n = bsz * seq
    nt = seq // tm
    has_vres = v_first is not None
    row = lambda b, i: (b * nt + i, 0)
    prev_map = lambda b, i: (jnp.maximum((b * nt + i) * (tm // 8) - 1, 0), 0)
    wide = pl.BlockSpec((tm, RWKV_WIDTH), row)
    ins = [p_rwkv, p_rwkv]
    specs = [pl.BlockSpec((tm, RWKV_IN), row), pl.BlockSpec((8, RWKV_IN), prev_map)]
    if has_vres:
        ins.append(v_first)
        specs.append(wide)
    names = ["mu", "w0", "wup", "a0", "aup", "gup", "kk", "ka", "rk"]
    if has_vres:
        names += ["vd", "vu", "vb"]
    for nm in names:
        ins.append(wts[nm])
        specs.append(_full(wts[nm].shape))
    for c in consts:
        ins.append(c)
        specs.append(_full(c.shape))
    o = jax.ShapeDtypeStruct((n, RWKV_WIDTH), F32)
    ob = jax.ShapeDtypeStruct((n, RWKV_WIDTH), BF16)
    nch = tm // CHUNK
    extra = 0 if has_vres else 1
    return pl.pallas_call(
        functools.partial(_rwkv_prep_kernel, tm=tm, has_vres=has_vres),
        grid=(bsz, nt),
        in_specs=specs,
        out_specs=[wide] * 9 + [pl.BlockSpec((nch, RWKV_WIDTH), lambda b, i: (b * nt + i, 0))] + [wide] * extra,
        out_shape=[ob] * 7 + [o] * 2 + [jax.ShapeDtypeStruct((n // CHUNK, RWKV_WIDTH), F32)] + [o] * extra,
        compiler_params=_params(("parallel", "parallel")),
        name="rwkv_prep",
    )(*ins)


def _rwkv_scan_kernel(rh_ref, ah_ref, bh_ref, kh_ref, bt_ref, kt_ref, v_ref, bon_ref, g_ref, gam_ref,
                      lng_ref, lnb_ref, o_ref, s_ref, *, nch):
    c = pl.program_id(1)

    @pl.when(c == 0)
    def _():
        s_ref[...] = jnp.zeros_like(s_ref)

    C = CHUNK
    rr = lax.broadcasted_iota(jnp.int32, (C, C), 0)
    cc = lax.broadcasted_iota(jnp.int32, (C, C), 1)
    strict = rr > cc
    incl = rr >= cc
    eye = jnp.where(rr == cc, 1.0, 0.0).astype(F32)
    heads = range(RWKV_HEADS)
    sls = [slice(h * RWKV_HEAD, (h + 1) * RWKV_HEAD) for h in heads]
    pre = []
    for ci in range(nch):
        rows = slice(ci * C, (ci + 1) * C)
        ar = [jnp.concatenate([ah_ref[rows, sl], rh_ref[rows, sl]], axis=0) for sl in sls]
        bk = [jnp.concatenate([bh_ref[rows, sl], kh_ref[rows, sl]], axis=0) for sl in sls]
        gm = [_dot(ar[h], bk[h], NT) for h in heads]
        a_ab = [jnp.where(strict, g[:C, :C], 0.0) for g in gm]
        a_ak = [jnp.where(strict, g[:C, C:], 0.0) for g in gm]
        a_r = [jnp.concatenate([jnp.where(incl, g[C:, :C], 0.0), jnp.where(incl, g[C:, C:], 0.0)], axis=1)
               for g in gm]
        pre.append((ar, a_ab, a_ak, a_r))
    pw = [p[1] for p in pre]
    tinv = [[eye + a for a in p[1]] for p in pre]
    for _ in range(int(math.log2(C)) - 1):
        pw = [[_dot(p, p) for p in prow] for prow in pw]
        tinv = [[t + _dot(p, t) for p, t in zip(prow, trow)] for prow, trow in zip(pw, tinv)]
    s = [s_ref[h] for h in heads]
    for ci in range(nch):
        rows = slice(ci * C, (ci + 1) * C)
        ar, _, a_ak, a_r = pre[ci]
        gam = gam_ref[pl.ds((c * nch) % 8 + ci, 1), :]
        vv = [v_ref[rows, sl] for sl in sls]
        xs = [_dot(ar[h], s[h], NT) for h in heads]
        z = [xs[h][:C] + _dot(a_ak[h], vv[h]) for h in heads]
        u = [_dot(tinv[ci][h], z[h]) for h in heads]
        uv = [jnp.concatenate([u[h].astype(BF16), vv[h]], axis=0) for h in heads]
        o = [xs[h][C:] + _dot(a_r[h], uv[h]) for h in heads]
        btk = [jnp.concatenate([bt_ref[rows, sl], kt_ref[rows, sl]], axis=0) for sl in sls]
        s = [s[h] * gam[:, sls[h]] + _dot(uv[h], btk[h], TN) for h in heads]
        for h, sl in enumerate(sls):
            oc = o[h] - jnp.mean(o[h], axis=-1, keepdims=True)
            on = oc * lax.rsqrt(jnp.mean(oc * oc, axis=-1, keepdims=True) + RWKV_GN_EPS)
            y = on * lng_ref[:, sl] + lnb_ref[:, sl]
            o_ref[rows, sl] = (y + bon_ref[rows, sl]) * g_ref[rows, sl]
    for h in heads:
        s_ref[h] = s[h]


def _rwkv_scan(prep, ln_g, ln_b, bsz, seq, nch):
    n = bsz * seq
    tb = nch * CHUNK
    nc = seq // tb
    row = lambda b, c: (b * nc + c, 0)
    wide = pl.BlockSpec((tb, RWKV_WIDTH), row)
    return pl.pallas_call(
        functools.partial(_rwkv_scan_kernel, nch=nch),
        grid=(bsz, nc),
        in_specs=[wide] * 9 + [pl.BlockSpec((8, RWKV_WIDTH), lambda b, c: ((b * nc + c) * nch // 8, 0)),
                               _full((1, RWKV_WIDTH)), _full((1, RWKV_WIDTH))],
        out_specs=wide,
        out_shape=jax.ShapeDtypeStruct((n, RWKV_WIDTH), F32),
        scratch_shapes=[pltpu.VMEM((RWKV_HEADS, RWKV_HEAD, RWKV_HEAD), F32)],
        compiler_params=_params(("parallel", "arbitrary")),
        name="rwkv_scan",
    )(*prep, ln_g, ln_b)


def _rwkv_consts(tm):
    t = jnp.arange(tm)
    same = (t[:, None] // CHUNK) == (t[None, :] // CHUNK)
    tri = (same & (t[:, None] >= t[None, :])).astype(BF16)
    blk = same.astype(BF16)
    sel = ((jnp.arange(tm // CHUNK)[:, None]) == (t[None, :] // CHUNK)).astype(BF16)
    ch = jnp.arange(RWKV_WIDTH) // RWKV_HEAD
    bd = (ch[:, None] == ch[None, :]).astype(BF16)
    return tri, blk, sel, bd


def _prep_s5(a_re, a_im, log_dt, b_re, b_im, c_re, c_im, d, bsz):
    dt = jnp.exp(log_dt.astype(F32))[:, None]
    lam_re = jnp.minimum(a_re.astype(F32), -1e-4)
    lam_im = a_im.astype(F32)
    mag = jnp.exp(dt * lam_re)
    ab_re, ab_im = mag * jnp.cos(dt * lam_im), mag * jnp.sin(dt * lam_im)
    den = lam_re * lam_re + lam_im * lam_im
    f_re = ((ab_re - 1.0) * lam_re + ab_im * lam_im) / den
    f_im = (ab_im * lam_re - (ab_re - 1.0) * lam_im) / den
    bb_re = f_re[..., None] * b_re - f_im[..., None] * b_im
    bb_im = f_re[..., None] * b_im + f_im[..., None] * b_re
    eye = jnp.eye(S5_GROUPS, dtype=F32)

    def expand_b(bb):
        return jnp.einsum('gpc,gh->gchp', bb, eye).reshape(S5_WIDTH, S5_LANES)

    def expand_c(cm):
        return jnp.einsum('gcp,gh->gphc', cm.astype(F32), eye).reshape(S5_LANES, S5_WIDTH)

    tile = lambda a: jnp.broadcast_to(a.reshape(1, S5_LANES), (bsz, S5_LANES))
    return (expand_b(bb_re).astype(BF16), expand_b(bb_im).astype(BF16), tile(ab_re), tile(ab_im),
            expand_c(c_re).astype(BF16), expand_c(c_im).astype(BF16), d.astype(F32).reshape(1, S5_WIDTH))


def _s5_kernel(u_ref, bre_ref, bim_ref, are_ref, aim_ref, cre_ref, cim_ref, d_ref, wglu_ref, bglu_ref,
               o_ref, xre_s, xim_s, st_re, st_im, *, ts, bsz):
    @pl.when(pl.program_id(0) == 0)
    def _():
        st_re[...] = jnp.zeros_like(st_re)
        st_im[...] = jnp.zeros_like(st_im)

    u = u_ref[...]
    ub = u.astype(BF16)
    xre_s[...] = _dg(ub, bre_ref[...])
    xim_s[...] = _dg(ub, bim_ref[...])

    def step(t, carry):
        x_re, x_im = carry
        a_re = are_ref[...]
        a_im = aim_ref[...]
        r0 = pl.multiple_of(t * bsz, bsz)
        n_re = a_re * x_re - a_im * x_im + xre_s[pl.ds(r0, bsz), :]
        n_im = a_re * x_im + a_im * x_re + xim_s[pl.ds(r0, bsz), :]
        xre_s[pl.ds(r0, bsz), :] = n_re
        xim_s[pl.ds(r0, bsz), :] = n_im
        return n_re, n_im

    x_re, x_im = lax.fori_loop(0, ts, step, (st_re[...], st_im[...]))
    st_re[...] = x_re
    st_im[...] = x_im
    y = _dot(xre_s[...], cre_ref[...]) - _dot(xim_s[...], cim_ref[...]) + d_ref[...] * u
    y = _gelu_tanh(y)
    o_ref[...] = y * _sigmoid(_dot(y, wglu_ref[...]) + bglu_ref[...])


def _s5(u_tm, s5w, w_glu, b_glu, bsz, seq, ts):
    rows = ts * bsz
    bre, bim, are, aim, cre, cim, d = s5w
    tile = pl.BlockSpec((rows, S5_WIDTH), lambda i: (i, 0))
    return pl.pallas_call(
        functools.partial(_s5_kernel, ts=ts, bsz=bsz),
        grid=(seq // ts,),
        in_specs=[tile, _full(bre.shape), _full(bim.shape), _full(are.shape), _full(aim.shape),
                  _full(cre.shape), _full(cim.shape), _full(d.shape), _full(w_glu.shape), _full(b_glu.shape)],
        out_specs=tile,
        out_shape=jax.ShapeDtypeStruct((seq * bsz, S5_WIDTH), F32),
        scratch_shapes=[pltpu.VMEM((rows, S5_LANES), F32), pltpu.VMEM((rows, S5_LANES), F32),
                        pltpu.VMEM((bsz, S5_LANES), F32), pltpu.VMEM((bsz, S5_LANES), F32)],
        compiler_params=_params(("arbitrary",)),
        name="s5_scan",
    )(u_tm, bre, bim, are, aim, cre, cim, d, w_glu, b_glu)


def _merge_kernel(*refs, with_router):
    if with_router:
        (x_ref, ym_ref, yr_ref, ys_ref, gt_ref, wb_ref, wo_ref, g_ref, b_ref, rt_ref,
         o_ref, ob_ref, lg_ref) = refs
    else:
        x_ref, ym_ref, yr_ref, ys_ref, gt_ref, wb_ref, wo_ref, g_ref, b_ref, o_ref = refs
    mixed = None
    for n, y_ref in enumerate((ym_ref, yr_ref, ys_ref)):
        t = gt_ref[:, n * D_MODEL:(n + 1) * D_MODEL].astype(F32) * _dot(y_ref[...], wb_ref[n])
        mixed = t if mixed is None else mixed + t
    z = DEEPNORM_ALPHA * x_ref[...] + _dot(mixed, wo_ref[...])
    out = _layer_norm(z, g_ref[...], b_ref[...])
    o_ref[...] = out
    if with_router:
        ob_ref[...] = out.astype(BF16)
        lg_ref[...] = _dot3(out, rt_ref[...])


def _merge(x2, y_mla, y_rwkv, y_s5_tm, gates, wb, wo, g, b, router, bsz, seq, tm):
    n = bsz * seq
    nt = seq // tm
    row = lambda bb, i: (bb * nt + i, 0)
    with_router = router is not None
    ins = [x2, y_mla, y_rwkv, y_s5_tm, gates, wb, wo, g, b]
    specs = [pl.BlockSpec((tm, D_MODEL), row),
             pl.BlockSpec((tm, 512), row), pl.BlockSpec((tm, 512), row),
             pl.BlockSpec((tm, S5_WIDTH), lambda bb, i: (i, bb)),
             pl.BlockSpec((tm, N_BRANCH * D_MODEL), row),
             _full(wb.shape), _full(wo.shape), _full(g.shape), _full(b.shape)]
    out_specs = [pl.BlockSpec((tm, D_MODEL), row)]
    out_shape = [jax.ShapeDtypeStruct((n, D_MODEL), F32)]
    if with_router:
        ins.append(router)
        specs.append(_full(router.shape))
        out_specs += [pl.BlockSpec((tm, D_MODEL), row), pl.BlockSpec((tm, LANE), row)]
        out_shape += [jax.ShapeDtypeStruct((n, D_MODEL), BF16), jax.ShapeDtypeStruct((n, LANE), F32)]
    res = pl.pallas_call(
        functools.partial(_merge_kernel, with_router=with_router),
        grid=(bsz, nt),
        in_specs=specs,
        out_specs=out_specs,
        out_shape=out_shape,
        compiler_params=_params(("parallel", "parallel")),
        name="merge_ln1",
    )(*ins)
    return res if with_router else (res[0], None, None)


def _ffn_kernel(x_ref, wg_ref, wu_ref, wd_ref, g_ref, b_ref, o_ref, acc_ref):
    j = pl.program_id(1)
    x = x_ref[...]
    xb = x.astype(BF16)
    h = _silu(_dg(xb, wg_ref[...])) * _dg(xb, wu_ref[...])
    part = _dot(h, wd_ref[...])

    @pl.when(j == 0)
    def _():
        acc_ref[...] = part

    @pl.when(j > 0)
    def _():
        acc_ref[...] += part

    @pl.when(j == pl.num_programs(1) - 1)
    def _():
        o_ref[...] = _layer_norm(DEEPNORM_ALPHA * x + acc_ref[...], g_ref[...], b_ref[...])


def _ffn(x2, wg, wu, wd, g, b, tm, tf):
    n = x2.shape[0]
    return pl.pallas_call(
        _ffn_kernel,
        grid=(n // tm, FFN_DIM // tf),
        in_specs=[pl.BlockSpec((tm, D_MODEL), lambda i, j: (i, 0)),
                  pl.BlockSpec((D_MODEL, tf), lambda i, j: (0, j)),
                  pl.BlockSpec((D_MODEL, tf), lambda i, j: (0, j)),
                  pl.BlockSpec((tf, D_MODEL), lambda i, j: (j, 0)),
                  _full(g.shape), _full(b.shape)],
        out_specs=pl.BlockSpec((tm, D_MODEL), lambda i, j: (i, 0)),
        out_shape=jax.ShapeDtypeStruct((n, D_MODEL), F32),
        scratch_shapes=[pltpu.VMEM((tm, D_MODEL), F32)],
        compiler_params=_params(("parallel", "arbitrary")),
        name="ffn_ln2",
    )(x2, wg, wu, wd, g, b)


def _moe_kernel(be_ref, nu_ref, xs_ref, sw_ref, wg_ref, wu_ref, wd_ref, o_ref, acc_ref):
    i = pl.program_id(0)
    j = pl.program_id(1)

    @pl.when(i < nu_ref[0])
    def _():
        xb = xs_ref[...]
        h = _silu(_dg(xb, wg_ref[0])) * _dg(xb, wu_ref[0])
        part = _dot(h, wd_ref[0])

        @pl.when(j == 0)
        def _():
            acc_ref[...] = part

        @pl.when(j > 0)
        def _():
            acc_ref[...] += part

        @pl.when(j == pl.num_programs(1) - 1)
        def _():
            o_ref[...] = acc_ref[...] * sw_ref[...]

    @pl.when((i >= nu_ref[0]) & (j == pl.num_programs(1) - 1))
    def _():
        o_ref[...] = jnp.zeros_like(o_ref)


def _moe_experts(xs, slot_w, block_e, n_used, wg, wu, wd, tf):
    n_slots = xs.shape[0]
    n_blocks = n_slots // MOE_BLOCK
    nf = EXPERT_DIM // tf

    def blk(i, nu):
        return jnp.minimum(i, nu[0] - 1)

    def fcol(i, j, nu):
        return jnp.where(i < nu[0], j, nf - 1)

    grid_spec = pltpu.PrefetchScalarGridSpec(
        num_scalar_prefetch=2,
        grid=(n_blocks, nf),
        in_specs=[pl.BlockSpec((MOE_BLOCK, D_MODEL), lambda i, j, be, nu: (blk(i, nu), 0)),
                  pl.BlockSpec((MOE_BLOCK, 1), lambda i, j, be, nu: (blk(i, nu), 0)),
                  pl.BlockSpec((1, D_MODEL, tf), lambda i, j, be, nu: (be[blk(i, nu)], 0, fcol(i, j, nu))),
                  pl.BlockSpec((1, D_MODEL, tf), lambda i, j, be, nu: (be[blk(i, nu)], 0, fcol(i, j, nu))),
                  pl.BlockSpec((1, tf, D_MODEL), lambda i, j, be, nu: (be[blk(i, nu)], fcol(i, j, nu), 0))],
        out_specs=pl.BlockSpec((MOE_BLOCK, D_MODEL), lambda i, j, be, nu: (i, 0)),
        scratch_shapes=[pltpu.VMEM((MOE_BLOCK, D_MODEL), F32)],
    )
    return pl.pallas_call(
        _moe_kernel,
        grid_spec=grid_spec,
        out_shape=jax.ShapeDtypeStruct((n_slots, D_MODEL), F32),
        compiler_params=_params(("arbitrary", "arbitrary")),
        name="moe_experts",
    )(block_e, n_used, xs, slot_w, wg, wu, wd)


def _combine_kernel(x_ref, y0_ref, y1_ref, g_ref, b_ref, o_ref):
    z = DEEPNORM_ALPHA * x_ref[...] + (y0_ref[...] + y1_ref[...])
    o_ref[...] = _layer_norm(z, g_ref[...], b_ref[...])


def _combine(x2, y0, y1, g, b, tm):
    n = x2.shape[0]
    tile = pl.BlockSpec((tm, D_MODEL), lambda i: (i, 0))
    return pl.pallas_call(
        _combine_kernel,
        grid=(n // tm,),
        in_specs=[tile, tile, tile, _full(g.shape), _full(b.shape)],
        out_specs=tile,
        out_shape=jax.ShapeDtypeStruct((n, D_MODEL), F32),
        compiler_params=_params(("parallel",)),
        name="moe_combine_ln2",
    )(x2, y0, y1, g, b)


def _moe(h2, h2b, logits, wg, wu, wd, e_off, g, b, tm, tf):
    n_tok = h2.shape[0]
    n_pair = n_tok * TOP_K
    top_logit, top_e = lax.top_k(logits[:, :N_EXPERTS], TOP_K)
    top_w = jax.nn.softmax(top_logit, axis=-1)
    pair_e = top_e.reshape(-1).astype(jnp.int32)
    onehot = (pair_e[:, None] == jnp.arange(N_EXPERTS, dtype=jnp.int32)[None, :]).astype(jnp.int32)
    csum = jnp.cumsum(onehot, axis=0)
    counts = csum[-1]
    start = jnp.cumsum(counts) - counts
    padded = (counts + MOE_BLOCK - 1) // MOE_BLOCK * MOE_BLOCK
    pad_end = jnp.cumsum(padded)
    pad_start = pad_end - padded
    pair_slot = jnp.sum(onehot * (pad_start[None, :] + csum - 1), axis=1)
    n_blocks = n_pair // MOE_BLOCK + N_EXPERTS
    n_slots = n_blocks * MOE_BLOCK
    block_e = jnp.minimum(jnp.searchsorted(pad_end, jnp.arange(n_blocks, dtype=jnp.int32) * MOE_BLOCK,
                                           side='right'), N_EXPERTS - 1).astype(jnp.int32)
    n_used = (pad_end[-1] // MOE_BLOCK).astype(jnp.int32).reshape(1)
    pair_tok = jnp.arange(n_pair, dtype=jnp.int32) // TOP_K
    _, tok_sorted, w_sorted = lax.sort((pair_e, pair_tok, top_w.reshape(-1)), num_keys=1, is_stable=True)
    tail = n_slots - n_pair
    tok_pad = jnp.concatenate([tok_sorted, jnp.zeros((tail,), jnp.int32)])
    w_pad = jnp.concatenate([w_sorted, jnp.zeros((tail,), F32)])
    slot = jnp.arange(n_slots, dtype=jnp.int32)
    slot_e = jnp.repeat(block_e, MOE_BLOCK)
    valid = slot < jnp.repeat((pad_start + counts)[block_e], MOE_BLOCK)
    shift = pad_start - start
    slot_tok = jnp.zeros((n_slots,), jnp.int32)
    slot_w = jnp.zeros((n_slots,), F32)
    for e in range(N_EXPERTS):
        here = valid & (slot_e == e)
        slot_tok = jnp.where(here, jnp.roll(tok_pad, shift[e]), slot_tok)
        slot_w = jnp.where(here, jnp.roll(w_pad, shift[e]), slot_w)
    xs = jnp.take(h2b, slot_tok, axis=0, mode="clip")
    y = _moe_experts(xs, slot_w[:, None], block_e + e_off, n_used, wg, wu, wd, tf)
    ps = pair_slot.reshape(n_tok, TOP_K)
    y0 = jnp.take(y, ps[:, 0], axis=0, mode="clip")
    y1 = jnp.take(y, ps[:, 1], axis=0, mode="clip")
    return _combine(h2, y0, y1, g, b, tm)


def kernel(x, positions, w_in, mla_q_norm, mla_w_uq, mla_kv_norm, mla_w_ukv,
           rwkv_mu, rwkv_w0, rwkv_w_up, rwkv_a0, rwkv_a_up, rwkv_g_up, rwkv_k_k, rwkv_k_a,
           rwkv_r_k, rwkv_ln_g, rwkv_ln_b, rwkv_vres_down, rwkv_vres_up, rwkv_vres_b,
           s5_a_re, s5_a_im, s5_log_dt, s5_b_re, s5_b_im, s5_c_re, s5_c_im, s5_d, s5_w_glu, s5_b_glu,
           w_branch, w_out, ln1_g, ln1_b, ln2_g, ln2_b,
           ffn_w_gate, ffn_w_up, ffn_w_down, moe_router, moe_w_gate, moe_w_up, moe_w_down):
    bsz, seq, _ = x.shape
    n = bsz * seq
    depth = w_in.shape[0]
    tm = min(512, seq)
    tq = min(FLASH_TQ, seq)
    ts = min(64, seq)
    row = lambda a: a.reshape(1, -1)

    cos, sin = _rope_tables(positions, tm)
    w_in_p = _prep_w_in(w_in)
    consts = _rwkv_consts(tm)
    zpad = jnp.zeros((DECAY_LORA, RWKV_WIDTH), F32)
    stack = lambda w: w.astype(BF16).reshape((-1,) + w.shape[2:])
    moe_wg, moe_wu, moe_wd = stack(moe_w_gate), stack(moe_w_up), stack(moe_w_down)

    h = x.reshape(n, D_MODEL)
    v_first = None
    for l in range(depth):
        p_mla, p_rwkv, u_s5, gates = _in_proj(h, w_in_p[l], bsz, seq, tm)

        wq, wqr, wk, wv = _prep_mla_weights(mla_w_uq[l], mla_w_ukv[l])
        q, k, v = _mla_prep(p_mla, cos, sin, row(mla_q_norm[l]), row(mla_kv_norm[l]),
                            wq, wqr, wk, wv, bsz, seq, tm)
        y_mla = _flash(q, k, v, bsz, seq, tq, min(FLASH_TD, tq))

        wts = {
            "mu": row(rwkv_mu[l]), "w0": row(rwkv_w0[l]),
            "wup": jnp.concatenate([rwkv_w_up[l], zpad], 0).astype(BF16),
            "a0": row(rwkv_a0[l]),
            "aup": jnp.concatenate([zpad, rwkv_a_up[l]], 0).astype(BF16),
            "gup": rwkv_g_up[l].astype(BF16),
            "kk": row(rwkv_k_k[l]), "ka": row(rwkv_k_a[l]), "rk": row(rwkv_r_k[l]),
        }
        if l > 0:
            wts["vd"] = rwkv_vres_down[l - 1].astype(BF16)
            wts["vu"] = rwkv_vres_up[l - 1].astype(BF16)
            wts["vb"] = row(rwkv_vres_b[l - 1])
        prep = _rwkv_prep(p_rwkv, v_first, wts, consts, bsz, seq, tm)
        if l == 0:
            v_first = prep[10]
        prep = prep[:10]
        y_rwkv = _rwkv_scan(prep, row(rwkv_ln_g[l]), row(rwkv_ln_b[l]), bsz, seq, RWKV_NCH)

        s5w = _prep_s5(s5_a_re[l], s5_a_im[l], s5_log_dt[l], s5_b_re[l], s5_b_im[l],
                       s5_c_re[l], s5_c_im[l], s5_d[l], bsz)
        y_s5 = _s5(u_s5.reshape(seq * bsz, S5_WIDTH), s5w, s5_w_glu[l].astype(BF16), row(s5_b_glu[l]),
                   bsz, seq, ts)

        moe_layer = l % 2 == 1
        router = None
        if moe_layer:
            router = jnp.pad(moe_router[l // 2], ((0, 0), (0, LANE - N_EXPERTS)))
        h, hb, logits = _merge(h, y_mla, y_rwkv, y_s5.reshape(seq, bsz * S5_WIDTH), gates,
                               w_branch[l].astype(BF16), w_out[l].astype(BF16), row(ln1_g[l]), row(ln1_b[l]),
                               router, bsz, seq, tm)
        if moe_layer:
            h = _moe(h, hb, logits, moe_wg, moe_wu, moe_wd, N_EXPERTS * (l // 2),
                     row(ln2_g[l]), row(ln2_b[l]), tm, tf=896)
        else:
            h = _ffn(h, ffn_w_gate[l // 2].astype(BF16), ffn_w_up[l // 2].astype(BF16),
                     ffn_w_down[l // 2].astype(BF16), row(ln2_g[l]), row(ln2_b[l]), tm, tf=1408)
    return h.reshape(bsz, seq, D_MODEL)
```

```python
import functools
import math

import jax
import jax.numpy as jnp
from jax import lax
from jax.experimental import pallas as pl
from jax.experimental.pallas import tpu as pltpu

F32 = jnp.float32
BF16 = jnp.bfloat16

D_MODEL = 1024
MLA_HEADS = 8
MLA_NOPE = 64
MLA_ROPE = 32
MLA_V = 64
Q_LORA = 384
KV_LORA = 256
ROPE_THETA = 10000.0
NEG_INF = -1e30
RWKV_HEADS = 8
RWKV_HEAD = 64
RWKV_WIDTH = 512
DECAY_LORA = 64
ICLR_LORA = 64
GATE_LORA = 128
RWKV_GN_EPS = 64e-5
RWKV_IN = 3 * RWKV_WIDTH + DECAY_LORA + ICLR_LORA + GATE_LORA
S5_GROUP = 16
S5_GROUPS = 32
S5_WIDTH = 512
S5_STATE = 64
S5_LANES = S5_GROUPS * S5_STATE
N_BRANCH = 3
FFN_DIM = 2816
N_EXPERTS = 8
TOP_K = 2
EXPERT_DIM = 3584
MOE_BLOCK = 512
DEPTH = 4
DEEPNORM_ALPHA = (2 * DEPTH) ** 0.25

LANE = 128
MLA_PAD = 896
CHUNK = 64
FLASH_TQ = 1024
FLASH_TD = 512
RWKV_NCH = 4
VMEM_LIMIT = 56 * 1024 * 1024

NN = ((1,), (0,))
NT = ((1,), (1,))
TN = ((0,), (0,))


def _dg(a, b, dims=NN):
    return lax.dot_general(a, b, (dims, ((), ())), preferred_element_type=F32)


def _dot(a, b, dims=NN):
    return _dg(a.astype(BF16), b.astype(BF16), dims)


def _split2(a):
    hi = a.astype(BF16)
    lo = (a - hi.astype(F32)).astype(BF16)
    return hi, lo


def _split3(a):
    hi = a.astype(BF16)
    r1 = a - hi.astype(F32)
    mid = r1.astype(BF16)
    lo = (r1 - mid.astype(F32)).astype(BF16)
    return hi, mid, lo


def _dot3(a, b, dims=NN):
    ah, al = _split2(a)
    bh, bl = _split2(b)
    return _dg(ah, bh, dims) + (_dg(ah, bl, dims) + _dg(al, bh, dims))


def _dot_exact_lhs(c_bf16, x):
    hi, mid, lo = _split3(x)
    return _dg(c_bf16, hi) + (_dg(c_bf16, mid) + _dg(c_bf16, lo))


def _dot_exact_rhs(x, c_bf16):
    hi, mid, lo = _split3(x)
    return _dg(hi, c_bf16) + (_dg(mid, c_bf16) + _dg(lo, c_bf16))


def _layer_norm(z, g, b, eps=1e-5):
    zc = z - jnp.mean(z, axis=-1, keepdims=True)
    var = jnp.mean(zc * zc, axis=-1, keepdims=True)
    return zc * lax.rsqrt(var + eps) * g + b


def _rms(x, g, eps=1e-6):
    return x * lax.rsqrt(jnp.mean(x * x, axis=-1, keepdims=True) + eps) * g


def _sigmoid(x):
    return 1.0 / (1.0 + jnp.exp(-x))


def _silu(x):
    return x * _sigmoid(x)


def _softplus(z):
    return jnp.maximum(z, 0.0) + jnp.log(1.0 + jnp.exp(-jnp.abs(z)))


def _gelu_tanh(x):
    c = math.sqrt(2.0 / math.pi)
    return x * (0.5 * (1.0 + jnp.tanh(c * (x + 0.044715 * (x * x * x)))))


def _params(sem, vmem=VMEM_LIMIT):
    return pltpu.CompilerParams(dimension_semantics=sem, vmem_limit_bytes=vmem)


def _full(shape):
    nd = len(shape)
    return pl.BlockSpec(shape, lambda *_: (0,) * nd, pipeline_mode=pl.Buffered(1))


def _rope_kernel(pos_ref, freq_ref, c_ref, s_ref):
    ang = pos_ref[...].astype(F32) * freq_ref[...]
    lane = lax.broadcasted_iota(jnp.int32, ang.shape, 1)
    c_ref[...] = jnp.where(lane < MLA_NOPE + MLA_ROPE, jnp.cos(ang), 0.0)
    s_ref[...] = jnp.sin(ang)


def _rope_tables(positions, tm):
    n = positions.size
    half = MLA_ROPE // 2
    inv_freq = ROPE_THETA ** (-jnp.arange(0, MLA_ROPE, 2, dtype=F32) / MLA_ROPE)
    freq = jnp.concatenate([jnp.zeros((MLA_NOPE,), F32), inv_freq, inv_freq,
                            jnp.zeros((LANE - MLA_NOPE - 2 * half,), F32)])[None, :]
    out = jax.ShapeDtypeStruct((n, LANE), F32)
    return pl.pallas_call(
        _rope_kernel,
        grid=(n // tm,),
        in_specs=[pl.BlockSpec((tm, 1), lambda i: (i, 0)), _full((1, LANE))],
        out_specs=[pl.BlockSpec((tm, LANE), lambda i: (i, 0))] * 2,
        out_shape=[out, out],
        compiler_params=_params(("parallel",)),
        name="rope_tables",
    )(positions.reshape(n, 1), freq)


IN_SPLITS = (MLA_PAD, RWKV_IN, S5_WIDTH, N_BRANCH * D_MODEL)
IN_PAD = sum(IN_SPLITS)


def _prep_w_in(w):
    cq = w[..., :Q_LORA]
    ckv = w[..., Q_LORA:Q_LORA + KV_LORA]
    kpe = w[..., Q_LORA + KV_LORA:Q_LORA + KV_LORA + MLA_ROPE]
    half = MLA_ROPE // 2
    k1, k2 = kpe[..., :half], kpe[..., half:]
    z64 = jnp.zeros(w.shape[:-1] + (MLA_NOPE,), w.dtype)
    z32 = jnp.zeros(w.shape[:-1] + (LANE - MLA_NOPE - MLA_ROPE,), w.dtype)
    rest = w[..., Q_LORA + KV_LORA + MLA_ROPE:]
    return jnp.concatenate([cq, ckv, z64, k1, k2, z32, z64, -k2, k1, z32, rest], axis=-1).astype(BF16)


def _in_proj_kernel(x_ref, w_ref, mla_ref, rwkv_ref, s5_ref, gate_ref):
    x = x_ref[...].astype(BF16)
    o0 = 0
    o1 = o0 + MLA_PAD
    o2 = o1 + RWKV_IN
    o3 = o2 + S5_WIDTH
    mla_ref[...] = _dg(x, w_ref[:, o0:o1])
    rwkv_ref[...] = _dg(x, w_ref[:, o1:o2])
    s5_ref[...] = _dg(x, w_ref[:, o2:o3])
    for n in range(N_BRANCH):
        c0 = o3 + n * D_MODEL
        gate_ref[:, n * D_MODEL:(n + 1) * D_MODEL] = _sigmoid(_dg(x, w_ref[:, c0:c0 + D_MODEL])).astype(BF16)


def _in_proj(x2, w, bsz, seq, tm):
    n = bsz * seq
    nt = seq // tm
    row = lambda b, i: (b * nt + i, 0)
    return pl.pallas_call(
        _in_proj_kernel,
        grid=(bsz, nt),
        in_specs=[pl.BlockSpec((tm, D_MODEL), row), _full((D_MODEL, IN_PAD))],
        out_specs=[pl.BlockSpec((tm, MLA_PAD), row),
                   pl.BlockSpec((tm, RWKV_IN), row),
                   pl.BlockSpec((tm, S5_WIDTH), lambda b, i: (i, b)),
                   pl.BlockSpec((tm, N_BRANCH * D_MODEL), row)],
        out_shape=[jax.ShapeDtypeStruct((n, MLA_PAD), F32),
                   jax.ShapeDtypeStruct((n, RWKV_IN), F32),
                   jax.ShapeDtypeStruct((seq, bsz * S5_WIDTH), F32),
                   jax.ShapeDtypeStruct((n, N_BRANCH * D_MODEL), BF16)],
        compiler_params=_params(("parallel", "parallel")),
        name="in_proj",
    )(x2, w)


def _prep_mla_weights(w_uq, w_ukv):
    half = MLA_ROPE // 2
    wq = w_uq.reshape(Q_LORA, MLA_HEADS, MLA_NOPE + MLA_ROPE)
    nope, pe1, pe2 = wq[..., :MLA_NOPE], wq[..., MLA_NOPE:MLA_NOPE + half], wq[..., MLA_NOPE + half:]
    z32 = jnp.zeros((Q_LORA, MLA_HEADS, LANE - MLA_NOPE - MLA_ROPE), w_uq.dtype)
    z64 = jnp.zeros((Q_LORA, MLA_HEADS, MLA_NOPE), w_uq.dtype)
    wq_main = jnp.concatenate([nope, pe1, pe2, z32], -1).reshape(Q_LORA, MLA_HEADS * LANE)
    wq_rot = jnp.concatenate([z64, -pe2, pe1, z32], -1).reshape(Q_LORA, MLA_HEADS * LANE)
    wkv = w_ukv.reshape(KV_LORA, MLA_HEADS, MLA_NOPE + MLA_V)
    k_nope, v = wkv[..., :MLA_NOPE], wkv[..., MLA_NOPE:]
    zk = jnp.zeros((KV_LORA, MLA_HEADS, LANE - MLA_NOPE), w_ukv.dtype)
    wk = jnp.concatenate([k_nope, zk], -1).reshape(KV_LORA, MLA_HEADS * LANE)
    zv = jnp.zeros_like(v)
    even = (jnp.arange(MLA_HEADS) % 2 == 0)[None, :, None]
    wv = jnp.concatenate([jnp.where(even, v, zv), jnp.where(even, zv, v)], -1).reshape(KV_LORA, MLA_HEADS * LANE)
    return wq_main.astype(BF16), wq_rot.astype(BF16), wk.astype(BF16), wv.astype(BF16)


def _mla_prep_kernel(p_ref, c_ref, s_ref, qn_ref, kvn_ref, wq_ref, wqr_ref, wk_ref, wv_ref,
                     q_out, k_out, v_out):
    cos = c_ref[...]
    sin = s_ref[...]
    qn = _rms(p_ref[:, :Q_LORA], qn_ref[...]).astype(BF16)
    kvn = _rms(p_ref[:, Q_LORA:Q_LORA + KV_LORA], kvn_ref[...]).astype(BF16)
    o = Q_LORA + KV_LORA
    k_rope = p_ref[:, o:o + LANE] * cos + p_ref[:, o + LANE:o + 2 * LANE] * sin
    scale = (MLA_NOPE + MLA_ROPE) ** -0.5 * math.log2(math.e)
    lane = lax.broadcasted_iota(jnp.int32, (1, LANE), 1)
    for h in range(MLA_HEADS):
        cols = slice(h * LANE, (h + 1) * LANE)
        q = _dg(qn, wq_ref[:, cols]) * cos + _dg(qn, wqr_ref[:, cols]) * sin
        q_out[0, h] = (q * scale).astype(BF16)
        k_out[0, h] = (_dg(kvn, wk_ref[:, cols]) + k_rope).astype(BF16)
        ones = jnp.where(lane == (1 - h % 2) * MLA_V, 1.0, 0.0)
        v_out[0, h] = (_dg(kvn, wv_ref[:, cols]) + ones).astype(BF16)


def _mla_prep(p_mla, cos, sin, q_norm, kv_norm, wq, wqr, wk, wv, bsz, seq, tm):
    nt = seq // tm
    row = lambda b, i: (b * nt + i, 0)
    hw = MLA_HEADS * LANE
    out = jax.ShapeDtypeStruct((bsz, MLA_HEADS, seq, LANE), BF16)
    ospec = pl.BlockSpec((1, MLA_HEADS, tm, LANE), lambda b, i: (b, 0, i, 0))
    return pl.pallas_call(
        _mla_prep_kernel,
        grid=(bsz, nt),
        in_specs=[pl.BlockSpec((tm, MLA_PAD), row),
                  pl.BlockSpec((tm, LANE), row), pl.BlockSpec((tm, LANE), row),
                  _full((1, Q_LORA)), _full((1, KV_LORA)),
                  _full((Q_LORA, hw)), _full((Q_LORA, hw)), _full((KV_LORA, hw)), _full((KV_LORA, hw))],
        out_specs=[ospec, ospec, ospec],
        out_shape=[out, out, out],
        compiler_params=_params(("parallel", "parallel")),
        name="mla_prep",
    )(p_mla, cos, sin, q_norm, kv_norm, wq, wqr, wk, wv)


def _flash_kernel(q_ref, k_ref, v_ref, o_ref, *, tq, td):
    i = pl.program_id(2)
    lane = lax.broadcasted_iota(jnp.int32, (tq, LANE), 1)
    qs = [q_ref[0, hh] for hh in range(2)]

    def scores(hh, start, size, r0=0):
        kj = k_ref[0, hh, pl.ds(start, size), :]
        vj = v_ref[0, hh, pl.ds(start, size), :]
        return _dg(qs[hh][r0:], kj, NT), vj

    def update(carry, s, vj):
        m, acc = carry
        m_new = jnp.maximum(m, jnp.max(s, axis=-1, keepdims=True))
        p = jnp.exp2(s - m_new)
        acc = jnp.exp2(m - m_new) * acc + _dg(p.astype(BF16), vj)
        return m_new, acc

    def body(j, carry):
        start = pl.multiple_of(j * tq, tq)
        return tuple(update(carry[hh], *scores(hh, start, tq)) for hh in range(2))

    init = (jnp.full((tq, 1), NEG_INF, F32), jnp.zeros((tq, LANE), F32))
    carry = lax.fori_loop(0, i, body, (init, init))
    for d in range(tq // td):
        r0 = d * td
        start = pl.multiple_of(i * tq + r0, td)
        causal = (lax.broadcasted_iota(jnp.int32, (tq - r0, td), 0)
                  >= lax.broadcasted_iota(jnp.int32, (tq - r0, td), 1))
        stepped = []
        for hh in range(2):
            m, acc = carry[hh]
            s, vj = scores(hh, start, td, r0)
            m_low, acc_low = update((m[r0:], acc[r0:]), jnp.where(causal, s, NEG_INF), vj)
            if r0:
                m_low = jnp.concatenate([m[:r0], m_low], axis=0)
                acc_low = jnp.concatenate([acc[:r0], acc_low], axis=0)
            stepped.append((m_low, acc_low))
        carry = tuple(stepped)
    out = None
    for hh in range(2):
        acc = carry[hh][1]
        ones_lane = (1 - hh) * MLA_V
        mine = (lane >= hh * MLA_V) & (lane < (hh + 1) * MLA_V)
        l = jnp.sum(jnp.where(lane == ones_lane, acc, 0.0), axis=-1, keepdims=True)
        o = jnp.where(mine, acc / l, 0.0)
        out = o if out is None else out + o
    o_ref[...] = out


def _flash(q, k, v, bsz, seq, tq, td):
    nt = seq // tq
    return pl.pallas_call(
        functools.partial(_flash_kernel, tq=tq, td=td),
        grid=(bsz, MLA_HEADS // 2, nt),
        in_specs=[pl.BlockSpec((1, 2, tq, LANE), lambda b, h, i: (b, h, i, 0)),
                  pl.BlockSpec((1, 2, seq, LANE), lambda b, h, i: (b, h, 0, 0)),
                  pl.BlockSpec((1, 2, seq, LANE), lambda b, h, i: (b, h, 0, 0))],
        out_specs=pl.BlockSpec((tq, LANE), lambda b, h, i: (b * nt + i, h)),
        out_shape=jax.ShapeDtypeStruct((bsz * seq, MLA_HEADS * MLA_V), F32),
        compiler_params=_params(("parallel", "parallel", "arbitrary")),
        name="mla_flash",
    )(q, k, v)


def _rwkv_prep_kernel(*refs, tm, has_vres):
    if has_vres:
        (p_ref, prev_ref, vf_ref, mu_ref, w0_ref, wup_ref, a0_ref, aup_ref, gup_ref, kk_ref, ka_ref,
         rk_ref, vd_ref, vu_ref, vb_ref, tri_ref, blk_ref, sel_ref, bd_ref,
         rh_o, ah_o, bh_o, kh_o, bt_o, kt_o, v_o, bon_o, g_o, gam_o) = refs
    else:
        (p_ref, prev_ref, mu_ref, w0_ref, wup_ref, a0_ref, aup_ref, gup_ref, kk_ref, ka_ref,
         rk_ref, tri_ref, blk_ref, sel_ref, bd_ref,
         rh_o, ah_o, bh_o, kh_o, bt_o, kt_o, v_o, bon_o, g_o, gam_o, vf_o) = refs
    i = pl.program_id(1)
    p = p_ref[...]
    first = jnp.where(i == 0, 0.0, prev_ref[7:8, :])
    rows = lax.broadcasted_iota(jnp.int32, p.shape, 0)
    prev = jnp.where(rows == 0, first, pltpu.roll(p, 1, axis=0))
    ps = p + (prev - p) * mu_ref[...]
    W = RWKV_WIDTH
    r, k, v = ps[:, :W], ps[:, W:2 * W], ps[:, 2 * W:3 * W]
    lora = ps[:, 3 * W:3 * W + DECAY_LORA + ICLR_LORA]
    g_d = ps[:, 3 * W + DECAY_LORA + ICLR_LORA:]
    w_log = -_softplus(-(w0_ref[...] + _dot(jnp.tanh(lora), wup_ref[...]))) - 0.5
    lw = -jnp.exp(w_log)
    iclr = _sigmoid(a0_ref[...] + _dot(lora, aup_ref[...]))
    g_o[...] = _dot(_sigmoid(g_d), gup_ref[...])
    if has_vres:
        mix = _sigmoid(vb_ref[...] + _dot(_dot(v, vd_ref[...]), vu_ref[...]))
        v = v + (vf_ref[...] - v) * mix
    bd = bd_ref[...]
    kk = k * kk_ref[...]
    kk = kk / jnp.maximum(jnp.sqrt(_dot_exact_rhs(kk * kk, bd)), 1e-12)
    k_h = k * (1.0 + (iclr - 1.0) * ka_ref[...])
    bon_o[...] = _dot_exact_rhs(r * k_h * rk_ref[...], bd) * v
    v_o[...] = v.astype(BF16)
    if not has_vres:
        vf_o[...] = v
    cum = _dot_exact_lhs(tri_ref[...], lw)
    tot = _dot_exact_lhs(blk_ref[...], lw)
    b = kk * iclr
    e_neg = jnp.exp(-cum)
    e_rem = jnp.exp(tot - cum)
    rh_o[...] = (r * jnp.exp(cum)).astype(BF16)
    ah_o[...] = (-kk * jnp.exp(cum - lw)).astype(BF16)
    bh_o[...] = (b * e_neg).astype(BF16)
    kh_o[...] = (k_h * e_neg).astype(BF16)
    bt_o[...] = (b * e_rem).astype(BF16)
    kt_o[...] = (k_h * e_rem).astype(BF16)
    gam_o[...] = jnp.exp(_dot_exact_lhs(sel_ref[...], lw))


def _rwkv_prep(p_rwkv, v_first, wts, consts, bsz, seq, tm):
    n = bsz * seq
    nt = seq // tm
    has_vres = v_first is not None
    row = lambda b, i: (b * nt + i, 0)
    prev_map = lambda b, i: (jnp.maximum((b * nt + i) * (tm // 8) - 1, 0), 0)
    wide = pl.BlockSpec((tm, RWKV_WIDTH), row)
    ins = [p_rwkv, p_rwkv]
    specs = [pl.BlockSpec((tm, RWKV_IN), row), pl.BlockSpec((8, RWKV_IN), prev_map)]
    if has_vres:
        ins.append(v_first)
        specs.append(wide)
    names = ["mu", "w0", "wup", "a0", "aup", "gup", "kk", "ka", "rk"]
    if has_vres:
        names += ["vd", "vu", "vb"]
    for nm in names:
        ins.append(wts[nm])
        specs.append(_full(wts[nm].shape))
    for c in consts:
        ins.append(c)
        specs.append(_full(c.shape))
    o = jax.ShapeDtypeStruct((n, RWKV_WIDTH), F32)
    ob = jax.ShapeDtypeStruct((n, RWKV_WIDTH), BF16)
    nch = tm // CHUNK
    extra = 0 if has_vres else 1
    return pl.pallas_call(
        functools.partial(_rwkv_prep_kernel, tm=tm, has_vres=has_vres),
        grid=(bsz, nt),
        in_specs=specs,
        out_specs=[wide] * 9 + [pl.BlockSpec((nch, RWKV_WIDTH), lambda b, i: (b * nt + i, 0))] + [wide] * extra,
        out_shape=[ob] * 7 + [o] * 2 + [jax.ShapeDtypeStruct((n // CHUNK, RWKV_WIDTH), F32)] + [o] * extra,
        compiler_params=_params(("parallel", "parallel")),
        name="rwkv_prep",
    )(*ins)


def _rwkv_scan_kernel(rh_ref, ah_ref, bh_ref, kh_ref, bt_ref, kt_ref, v_ref, bon_ref, g_ref, gam_ref,
                      lng_ref, lnb_ref, o_ref, s_ref, *, nch):
    c = pl.program_id(1)

    @pl.when(c == 0)
    def _():
        s_ref[...] = jnp.zeros_like(s_ref)

    C = CHUNK
    rr = lax.broadcasted_iota(jnp.int32, (C, C), 0)
    cc = lax.broadcasted_iota(jnp.int32, (C, C), 1)
    strict = rr > cc
    incl = rr >= cc
    eye = jnp.where(rr == cc, 1.0, 0.0).astype(F32)
    heads = range(RWKV_HEADS)
    sls = [slice(h * RWKV_HEAD, (h + 1) * RWKV_HEAD) for h in heads]
    pre = []
    for ci in range(nch):
        rows = slice(ci * C, (ci + 1) * C)
        ar = [jnp.concatenate([ah_ref[rows, sl], rh_ref[rows, sl]], axis=0) for sl in sls]
        bk = [jnp.concatenate([bh_ref[rows, sl], kh_ref[rows, sl]], axis=0) for sl in sls]
        gm = [_dot(ar[h], bk[h], NT) for h in heads]
        a_ab = [jnp.where(strict, g[:C, :C], 0.0) for g in gm]
        a_ak = [jnp.where(strict, g[:C, C:], 0.0) for g in gm]
        a_r = [jnp.concatenate([jnp.where(incl, g[C:, :C], 0.0), jnp.where(incl, g[C:, C:], 0.0)], axis=1)
               for g in gm]
        pre.append((ar, a_ab, a_ak, a_r))
    pw = [p[1] for p in pre]
    tinv = [[eye + a for a in p[1]] for p in pre]
    for _ in range(int(math.log2(C)) - 1):
        pw = [[_dot(p, p) for p in prow] for prow in pw]
        tinv = [[t + _dot(p, t) for p, t in zip(prow, trow)] for prow, trow in zip(pw, tinv)]
    s = [s_ref[h] for h in heads]
    for ci in range(nch):
        rows = slice(ci * C, (ci + 1) * C)
        ar, _, a_ak, a_r = pre[ci]
        gam = gam_ref[pl.ds((c * nch) % 8 + ci, 1), :]
        vv = [v_ref[rows, sl] for sl in sls]
        xs = [_dot(ar[h], s[h], NT) for h in heads]
        z = [xs[h][:C] + _dot(a_ak[h], vv[h]) for h in heads]
        u = [_dot(tinv[ci][h], z[h]) for h in heads]
        uv = [jnp.concatenate([u[h].astype(BF16), vv[h]], axis=0) for h in heads]
        o = [xs[h][C:] + _dot(a_r[h], uv[h]) for h in heads]
        btk = [jnp.concatenate([bt_ref[rows, sl], kt_ref[rows, sl]], axis=0) for sl in sls]
        s = [s[h] * gam[:, sls[h]] + _dot(uv[h], btk[h], TN) for h in heads]
        for h, sl in enumerate(sls):
            oc = o[h] - jnp.mean(o[h], axis=-1, keepdims=True)
            on = oc * lax.rsqrt(jnp.mean(oc * oc, axis=-1, keepdims=True) + RWKV_GN_EPS)
            y = on * lng_ref[:, sl] + lnb_ref[:, sl]
            o_ref[rows, sl] = (y + bon_ref[rows, sl]) * g_ref[rows, sl]
    for h in heads:
        s_ref[h] = s[h]


def _rwkv_scan(prep, ln_g, ln_b, bsz, seq, nch):
    n = bsz * seq
    tb = nch * CHUNK
    nc = seq // tb
    row = lambda b, c: (b * nc + c, 0)
    wide = pl.BlockSpec((tb, RWKV_WIDTH), row)
    return pl.pallas_call(
        functools.partial(_rwkv_scan_kernel, nch=nch),
        grid=(bsz, nc),
        in_specs=[wide] * 9 + [pl.BlockSpec((8, RWKV_WIDTH), lambda b, c: ((b * nc + c) * nch // 8, 0)),
                               _full((1, RWKV_WIDTH)), _full((1, RWKV_WIDTH))],
        out_specs=wide,
        out_shape=jax.ShapeDtypeStruct((n, RWKV_WIDTH), F32),
        scratch_shapes=[pltpu.VMEM((RWKV_HEADS, RWKV_HEAD, RWKV_HEAD), F32)],
        compiler_params=_params(("parallel", "arbitrary")),
        name="rwkv_scan",
    )(*prep, ln_g, ln_b)


def _rwkv_consts(tm):
    t = jnp.arange(tm)
    same = (t[:, None] // CHUNK) == (t[None, :] // CHUNK)
    tri = (same & (t[:, None] >= t[None, :])).astype(BF16)
    blk = same.astype(BF16)
    sel = ((jnp.arange(tm // CHUNK)[:, None]) == (t[None, :] // CHUNK)).astype(BF16)
    ch = jnp.arange(RWKV_WIDTH) // RWKV_HEAD
    bd = (ch[:, None] == ch[None, :]).astype(BF16)
    return tri, blk, sel, bd


def _prep_s5(a_re, a_im, log_dt, b_re, b_im, c_re, c_im, d, bsz):
    dt = jnp.exp(log_dt.astype(F32))[:, None]
    lam_re = jnp.minimum(a_re.astype(F32), -1e-4)
    lam_im = a_im.astype(F32)
    mag = jnp.exp(dt * lam_re)
    ab_re, ab_im = mag * jnp.cos(dt * lam_im), mag * jnp.sin(dt * lam_im)
    den = lam_re * lam_re + lam_im * lam_im
    f_re = ((ab_re - 1.0) * lam_re + ab_im * lam_im) / den
    f_im = (ab_im * lam_re - (ab_re - 1.0) * lam_im) / den
    bb_re = f_re[..., None] * b_re - f_im[..., None] * b_im
    bb_im = f_re[..., None] * b_im + f_im[..., None] * b_re
    eye = jnp.eye(S5_GROUPS, dtype=F32)

    def expand_b(bb):
        return jnp.einsum('gpc,gh->gchp', bb, eye).reshape(S5_WIDTH, S5_LANES)

    def expand_c(cm):
        return jnp.einsum('gcp,gh->gphc', cm.astype(F32), eye).reshape(S5_LANES, S5_WIDTH)

    tile = lambda a: jnp.broadcast_to(a.reshape(1, S5_LANES), (bsz, S5_LANES))
    return (expand_b(bb_re).astype(BF16), expand_b(bb_im).astype(BF16), tile(ab_re), tile(ab_im),
            expand_c(c_re).astype(BF16), expand_c(c_im).astype(BF16), d.astype(F32).reshape(1, S5_WIDTH))


def _s5_kernel(u_ref, bre_ref, bim_ref, are_ref, aim_ref, cre_ref, cim_ref, d_ref, wglu_ref, bglu_ref,
               o_ref, xre_s, xim_s, st_re, st_im, *, ts, bsz):
    @pl.when(pl.program_id(0) == 0)
    def _():
        st_re[...] = jnp.zeros_like(st_re)
        st_im[...] = jnp.zeros_like(st_im)

    u = u_ref[...]
    ub = u.astype(BF16)
    xre_s[...] = _dg(ub, bre_ref[...])
    xim_s[...] = _dg(ub, bim_ref[...])

    def step(t, carry):
        x_re, x_im = carry
        a_re = are_ref[...]
        a_im = aim_ref[...]
        r0 = pl.multiple_of(t * bsz, bsz)
        n_re = a_re * x_re - a_im * x_im + xre_s[pl.ds(r0, bsz), :]
        n_im = a_re * x_im + a_im * x_re + xim_s[pl.ds(r0, bsz), :]
        xre_s[pl.ds(r0, bsz), :] = n_re
        xim_s[pl.ds(r0, bsz), :] = n_im
        return n_re, n_im

    x_re, x_im = lax.fori_loop(0, ts, step, (st_re[...], st_im[...]))
    st_re[...] = x_re
    st_im[...] = x_im
    y = _dot(xre_s[...], cre_ref[...]) - _dot(xim_s[...], cim_ref[...]) + d_ref[...] * u
    y = _gelu_tanh(y)
    o_ref[...] = y * _sigmoid(_dot(y, wglu_ref[...]) + bglu_ref[...])


def _s5(u_tm, s5w, w_glu, b_glu, bsz, seq, ts):
    rows = ts * bsz
    bre, bim, are, aim, cre, cim, d = s5w
    tile = pl.BlockSpec((rows, S5_WIDTH), lambda i: (i, 0))
    return pl.pallas_call(
        functools.partial(_s5_kernel, ts=ts, bsz=bsz),
        grid=(seq // ts,),
        in_specs=[tile, _full(bre.shape), _full(bim.shape), _full(are.shape), _full(aim.shape),
                  _full(cre.shape), _full(cim.shape), _full(d.shape), _full(w_glu.shape), _full(b_glu.shape)],
        out_specs=tile,
        out_shape=jax.ShapeDtypeStruct((seq * bsz, S5_WIDTH), F32),
        scratch_shapes=[pltpu.VMEM((rows, S5_LANES), F32), pltpu.VMEM((rows, S5_LANES), F32),
                        pltpu.VMEM((bsz, S5_LANES), F32), pltpu.VMEM((bsz, S5_LANES), F32)],
        compiler_params=_params(("arbitrary",)),
        name="s5_scan",
    )(u_tm, bre, bim, are, aim, cre, cim, d, w_glu, b_glu)


def _merge_kernel(*refs, with_router):
    if with_router:
        (x_ref, ym_ref, yr_ref, ys_ref, gt_ref, wb_ref, wo_ref, g_ref, b_ref, rt_ref,
         o_ref, ob_ref, lg_ref) = refs
    else:
        x_ref, ym_ref, yr_ref, ys_ref, gt_ref, wb_ref, wo_ref, g_ref, b_ref, o_ref = refs
    mixed = None
    for n, y_ref in enumerate((ym_ref, yr_ref, ys_ref)):
        t = gt_ref[:, n * D_MODEL:(n + 1) * D_MODEL].astype(F32) * _dot(y_ref[...], wb_ref[n])
        mixed = t if mixed is None else mixed + t
    z = DEEPNORM_ALPHA * x_ref[...] + _dot(mixed, wo_ref[...])
    out = _layer_norm(z, g_ref[...], b_ref[...])
    o_ref[...] = out
    if with_router:
        ob_ref[...] = out.astype(BF16)
        lg_ref[...] = _dot3(out, rt_ref[...])


def _merge(x2, y_mla, y_rwkv, y_s5_tm, gates, wb, wo, g, b, router, bsz, seq, tm):
    n = bsz * seq
    nt = seq // tm
    row = lambda bb, i: (bb * nt + i, 0)
    with_router = router is not None
    ins = [x2, y_mla, y_rwkv, y_s5_tm, gates, wb, wo, g, b]
    specs = [pl.BlockSpec((tm, D_MODEL), row),
             pl.BlockSpec((tm, 512), row), pl.BlockSpec((tm, 512), row),
             pl.BlockSpec((tm, S5_WIDTH), lambda bb, i: (i, bb)),
             pl.BlockSpec((tm, N_BRANCH * D_MODEL), row),
             _full(wb.shape), _full(wo.shape), _full(g.shape), _full(b.shape)]
    out_specs = [pl.BlockSpec((tm, D_MODEL), row)]
    out_shape = [jax.ShapeDtypeStruct((n, D_MODEL), F32)]
    if with_router:
        ins.append(router)
        specs.append(_full(router.shape))
        out_specs += [pl.BlockSpec((tm, D_MODEL), row), pl.BlockSpec((tm, LANE), row)]
        out_shape += [jax.ShapeDtypeStruct((n, D_MODEL), BF16), jax.ShapeDtypeStruct((n, LANE), F32)]
    res = pl.pallas_call(
        functools.partial(_merge_kernel, with_router=with_router),
        grid=(bsz, nt),
        in_specs=specs,
        out_specs=out_specs,
        out_shape=out_shape,
        compiler_params=_params(("parallel", "parallel")),
        name="merge_ln1",
    )(*ins)
    return res if with_router else (res[0], None, None)


def _ffn_kernel(x_ref, wg_ref, wu_ref, wd_ref, g_ref, b_ref, o_ref, acc_ref):
    j = pl.program_id(1)
    x = x_ref[...]
    xb = x.astype(BF16)
    h = _silu(_dg(xb, wg_ref[...])) * _dg(xb, wu_ref[...])
    part = _dot(h, wd_ref[...])

    @pl.when(j == 0)
    def _():
        acc_ref[...] = part

    @pl.when(j > 0)
    def _():
        acc_ref[...] += part

    @pl.when(j == pl.num_programs(1) - 1)
    def _():
        o_ref[...] = _layer_norm(DEEPNORM_ALPHA * x + acc_ref[...], g_ref[...], b_ref[...])


def _ffn(x2, wg, wu, wd, g, b, tm, tf):
    n = x2.shape[0]
    return pl.pallas_call(
        _ffn_kernel,
        grid=(n // tm, FFN_DIM // tf),
        in_specs=[pl.BlockSpec((tm, D_MODEL), lambda i, j: (i, 0)),
                  pl.BlockSpec((D_MODEL, tf), lambda i, j: (0, j)),
                  pl.BlockSpec((D_MODEL, tf), lambda i, j: (0, j)),
                  pl.BlockSpec((tf, D_MODEL), lambda i, j: (j, 0)),
                  _full(g.shape), _full(b.shape)],
        out_specs=pl.BlockSpec((tm, D_MODEL), lambda i, j: (i, 0)),
        out_shape=jax.ShapeDtypeStruct((n, D_MODEL), F32),
        scratch_shapes=[pltpu.VMEM((tm, D_MODEL), F32)],
        compiler_params=_params(("parallel", "arbitrary")),
        name="ffn_ln2",
    )(x2, wg, wu, wd, g, b)


def _moe_kernel(be_ref, nu_ref, xs_ref, sw_ref, wg_ref, wu_ref, wd_ref, o_ref, acc_ref):
    i = pl.program_id(0)
    j = pl.program_id(1)

    @pl.when(i < nu_ref[0])
    def _():
        xb = xs_ref[...]
        h = _silu(_dg(xb, wg_ref[0])) * _dg(xb, wu_ref[0])
        part = _dot(h, wd_ref[0])

        @pl.when(j == 0)
        def _():
            acc_ref[...] = part

        @pl.when(j > 0)
        def _():
            acc_ref[...] += part

        @pl.when(j == pl.num_programs(1) - 1)
        def _():
            o_ref[...] = acc_ref[...] * sw_ref[...]

    @pl.when((i >= nu_ref[0]) & (j == pl.num_programs(1) - 1))
    def _():
        o_ref[...] = jnp.zeros_like(o_ref)


def _moe_experts(xs, slot_w, block_e, n_used, wg, wu, wd, tf):
    n_slots = xs.shape[0]
    n_blocks = n_slots // MOE_BLOCK
    nf = EXPERT_DIM // tf

    def blk(i, nu):
        return jnp.minimum(i, nu[0] - 1)

    def fcol(i, j, nu):
        return jnp.where(i < nu[0], j, nf - 1)

    grid_spec = pltpu.PrefetchScalarGridSpec(
        num_scalar_prefetch=2,
        grid=(n_blocks, nf),
        in_specs=[pl.BlockSpec((MOE_BLOCK, D_MODEL), lambda i, j, be, nu: (blk(i, nu), 0)),
                  pl.BlockSpec((MOE_BLOCK, 1), lambda i, j, be, nu: (blk(i, nu), 0)),
                  pl.BlockSpec((1, D_MODEL, tf), lambda i, j, be, nu: (be[blk(i, nu)], 0, fcol(i, j, nu))),
                  pl.BlockSpec((1, D_MODEL, tf), lambda i, j, be, nu: (be[blk(i, nu)], 0, fcol(i, j, nu))),
                  pl.BlockSpec((1, tf, D_MODEL), lambda i, j, be, nu: (be[blk(i, nu)], fcol(i, j, nu), 0))],
        out_specs=pl.BlockSpec((MOE_BLOCK, D_MODEL), lambda i, j, be, nu: (i, 0)),
        scratch_shapes=[pltpu.VMEM((MOE_BLOCK, D_MODEL), F32)],
    )
    return pl.pallas_call(
        _moe_kernel,
        grid_spec=grid_spec,
        out_shape=jax.ShapeDtypeStruct((n_slots, D_MODEL), F32),
        compiler_params=_params(("arbitrary", "arbitrary")),
        name="moe_experts",
    )(block_e, n_used, xs, slot_w, wg, wu, wd)


def _combine_kernel(x_ref, y0_ref, y1_ref, g_ref, b_ref, o_ref):
    z = DEEPNORM_ALPHA * x_ref[...] + (y0_ref[...] + y1_ref[...])
    o_ref[...] = _layer_norm(z, g_ref[...], b_ref[...])


def _combine(x2, y0, y1, g, b, tm):
    n = x2.shape[0]
    tile = pl.BlockSpec((tm, D_MODEL), lambda i: (i, 0))
    return pl.pallas_call(
        _combine_kernel,
        grid=(n // tm,),
        in_specs=[tile, tile, tile, _full(g.shape), _full(b.shape)],
        out_specs=tile,
        out_shape=jax.ShapeDtypeStruct((n, D_MODEL), F32),
        compiler_params=_params(("parallel",)),
        name="moe_combine_ln2",
    )(x2, y0, y1, g, b)


def _moe(h2, h2b, logits, wg, wu, wd, e_off, g, b, tm, tf):
    n_tok = h2.shape[0]
    n_pair = n_tok * TOP_K
    top_logit, top_e = lax.top_k(logits[:, :N_EXPERTS], TOP_K)
    top_w = jax.nn.softmax(top_logit, axis=-1)
    pair_e = top_e.reshape(-1).astype(jnp.int32)
    onehot = (pair_e[:, None] == jnp.arange(N_EXPERTS, dtype=jnp.int32)[None, :]).astype(jnp.int32)
    csum = jnp.cumsum(onehot, axis=0)
    counts = csum[-1]
    start = jnp.cumsum(counts) - counts
    padded = (counts + MOE_BLOCK - 1) // MOE_BLOCK * MOE_BLOCK
    pad_end = jnp.cumsum(padded)
    pad_start = pad_end - padded
    pair_slot = jnp.sum(onehot * (pad_start[None, :] + csum - 1), axis=1)
    n_blocks = n_pair // MOE_BLOCK + N_EXPERTS
    n_slots = n_blocks * MOE_BLOCK
    block_e = jnp.minimum(jnp.searchsorted(pad_end, jnp.arange(n_blocks, dtype=jnp.int32) * MOE_BLOCK,
                                           side='right'), N_EXPERTS - 1).astype(jnp.int32)
    n_used = (pad_end[-1] // MOE_BLOCK).astype(jnp.int32).reshape(1)
    pair_tok = jnp.arange(n_pair, dtype=jnp.int32) // TOP_K
    _, tok_sorted, w_sorted = lax.sort((pair_e, pair_tok, top_w.reshape(-1)), num_keys=1, is_stable=True)
    tail = n_slots - n_pair
    tok_pad = jnp.concatenate([tok_sorted, jnp.zeros((tail,), jnp.int32)])
    w_pad = jnp.concatenate([w_sorted, jnp.zeros((tail,), F32)])
    slot = jnp.arange(n_slots, dtype=jnp.int32)
    slot_e = jnp.repeat(block_e, MOE_BLOCK)
    valid = slot < jnp.repeat((pad_start + counts)[block_e], MOE_BLOCK)
    shift = pad_start - start
    slot_tok = jnp.zeros((n_slots,), jnp.int32)
    slot_w = jnp.zeros((n_slots,), F32)
    for e in range(N_EXPERTS):
        here = valid & (slot_e == e)
        slot_tok = jnp.where(here, jnp.roll(tok_pad, shift[e]), slot_tok)
        slot_w = jnp.where(here, jnp.roll(w_pad, shift[e]), slot_w)
    xs = jnp.take(h2b, slot_tok, axis=0, mode="clip")
    y = _moe_experts(xs, slot_w[:, None], block_e + e_off, n_used, wg, wu, wd, tf)
    ps = pair_slot.reshape(n_tok, TOP_K)
    y0 = jnp.take(y, ps[:, 0], axis=0, mode="clip")
    y1 = jnp.take(y, ps[:, 1], axis=0, mode="clip")
    return _combine(h2, y0, y1, g, b, tm)


def kernel(x, positions, w_in, mla_q_norm, mla_w_uq, mla_kv_norm, mla_w_ukv,
           rwkv_mu, rwkv_w0, rwkv_w_up, rwkv_a0, rwkv_a_up, rwkv_g_up, rwkv_k_k, rwkv_k_a,
           rwkv_r_k, rwkv_ln_g, rwkv_ln_b, rwkv_vres_down, rwkv_vres_up, rwkv_vres_b,
           s5_a_re, s5_a_im, s5_log_dt, s5_b_re, s5_b_im, s5_c_re, s5_c_im, s5_d, s5_w_glu, s5_b_glu,
           w_branch, w_out, ln1_g, ln1_b, ln2_g, ln2_b,
           ffn_w_gate, ffn_w_up, ffn_w_down, moe_router, moe_w_gate, moe_w_up, moe_w_down):
    bsz, seq, _ = x.shape
    n = bsz * seq
    depth = w_in.shape[0]
    tm = min(512, seq)
    tq = min(FLASH_TQ, seq)
    ts = min(64, seq)
    row = lambda a: a.reshape(1, -1)

    cos, sin = _rope_tables(positions, tm)
    w_in_p = _prep_w_in(w_in)
    consts = _rwkv_consts(tm)
    zpad = jnp.zeros((DECAY_LORA, RWKV_WIDTH), F32)
    stack = lambda w: w.astype(BF16).reshape((-1,) + w.shape[2:])
    moe_wg, moe_wu, moe_wd = stack(moe_w_gate), stack(moe_w_up), stack(moe_w_down)

    h = x.reshape(n, D_MODEL)
    v_first = None
    for l in range(depth):
        p_mla, p_rwkv, u_s5, gates = _in_proj(h, w_in_p[l], bsz, seq, tm)

        wq, wqr, wk, wv = _prep_mla_weights(mla_w_uq[l], mla_w_ukv[l])
        q, k, v = _mla_prep(p_mla, cos, sin, row(mla_q_norm[l]), row(mla_kv_norm[l]),
                            wq, wqr, wk, wv, bsz, seq, tm)
        y_mla = _flash(q, k, v, bsz, seq, tq, min(FLASH_TD, tq))

        wts = {
            "mu": row(rwkv_mu[l]), "w0": row(rwkv_w0[l]),
            "wup": jnp.concatenate([rwkv_w_up[l], zpad], 0).astype(BF16),
            "a0": row(rwkv_a0[l]),
            "aup": jnp.concatenate([zpad, rwkv_a_up[l]], 0).astype(BF16),
            "gup": rwkv_g_up[l].astype(BF16),
            "kk": row(rwkv_k_k[l]), "ka": row(rwkv_k_a[l]), "rk": row(rwkv_r_k[l]),
        }
        if l > 0:
            wts["vd"] = rwkv_vres_down[l - 1].astype(BF16)
            wts["vu"] = rwkv_vres_up[l - 1].astype(BF16)
            wts["vb"] = row(rwkv_vres_b[l - 1])
        prep = _rwkv_prep(p_rwkv, v_first, wts, consts, bsz, seq, tm)
        if l == 0:
            v_first = prep[10]
        prep = prep[:10]
        y_rwkv = _rwkv_scan(prep, row(rwkv_ln_g[l]), row(rwkv_ln_b[l]), bsz, seq, RWKV_NCH)

        s5w = _prep_s5(s5_a_re[l], s5_a_im[l], s5_log_dt[l], s5_b_re[l], s5_b_im[l],
                       s5_c_re[l], s5_c_im[l], s5_d[l], bsz)
        y_s5 = _s5(u_s5.reshape(seq * bsz, S5_WIDTH), s5w, s5_w_glu[l].astype(BF16), row(s5_b_glu[l]),
                   bsz, seq, ts)

        moe_layer = l % 2 == 1
        router = None
        if moe_layer:
            router = jnp.pad(moe_router[l // 2], ((0, 0), (0, LANE - N_EXPERTS)))
        h, hb, logits = _merge(h, y_mla, y_rwkv, y_s5.reshape(seq, bsz * S5_WIDTH), gates,
                               w_branch[l].astype(BF16), w_out[l].astype(BF16), row(ln1_g[l]), row(ln1_b[l]),
                               router, bsz, seq, tm)
        if moe_layer:
            h = _moe(h, hb, logits, moe_wg, moe_wu, moe_wd, N_EXPERTS * (l // 2),
                     row(ln2_g[l]), row(ln2_b[l]), tm, tf=1792)
        else:
            h = _ffn(h, ffn_w_gate[l // 2].astype(BF16), ffn_w_up[l // 2].astype(BF16),
                     ffn_w_down[l // 2].astype(BF16), row(ln2_g[l]), row(ln2_b[l]), tm, tf=1408)
    return h.reshape(bsz, seq, D_MODEL)
```

```python
import functools
import math

import jax
import jax.numpy as jnp
from jax import lax
from jax.experimental import pallas as pl
from jax.experimental.pallas import tpu as pltpu

F32 = jnp.float32
BF16 = jnp.bfloat16

D_MODEL = 1024
MLA_HEADS = 8
MLA_NOPE = 64
MLA_ROPE = 32
MLA_V = 64
Q_LORA = 384
KV_LORA = 256
ROPE_THETA = 10000.0
NEG_INF = -1e30
RWKV_HEADS = 8
RWKV_HEAD = 64
RWKV_WIDTH = 512
DECAY_LORA = 64
ICLR_LORA = 64
GATE_LORA = 128
RWKV_GN_EPS = 64e-5
RWKV_IN = 3 * RWKV_WIDTH + DECAY_LORA + ICLR_LORA + GATE_LORA
S5_GROUP = 16
S5_GROUPS = 32
S5_WIDTH = 512
S5_STATE = 64
S5_LANES = S5_GROUPS * S5_STATE
N_BRANCH = 3
FFN_DIM = 2816
N_EXPERTS = 8
TOP_K = 2
EXPERT_DIM = 3584
MOE_BLOCK = 512
DEPTH = 4
DEEPNORM_ALPHA = (2 * DEPTH) ** 0.25

LANE = 128
MLA_PAD = 896
CHUNK = 64
FLASH_TQ = 1024
FLASH_TD = 512
RWKV_NCH = 4
VMEM_LIMIT = 56 * 1024 * 1024

NN = ((1,), (0,))
NT = ((1,), (1,))
TN = ((0,), (0,))


def _dg(a, b, dims=NN):
    return lax.dot_general(a, b, (dims, ((), ())), preferred_element_type=F32)


def _dot(a, b, dims=NN):
    return _dg(a.astype(BF16), b.astype(BF16), dims)


def _split2(a):
    hi = a.astype(BF16)
    lo = (a - hi.astype(F32)).astype(BF16)
    return hi, lo


def _split3(a):
    hi = a.astype(BF16)
    r1 = a - hi.astype(F32)
    mid = r1.astype(BF16)
    lo = (r1 - mid.astype(F32)).astype(BF16)
    return hi, mid, lo


def _dot3(a, b, dims=NN):
    ah, al = _split2(a)
    bh, bl = _split2(b)
    return _dg(ah, bh, dims) + (_dg(ah, bl, dims) + _dg(al, bh, dims))


def _dot_exact_lhs(c_bf16, x):
    hi, mid, lo = _split3(x)
    return _dg(c_bf16, hi) + (_dg(c_bf16, mid) + _dg(c_bf16, lo))


def _dot_exact_rhs(x, c_bf16):
    hi, mid, lo = _split3(x)
    return _dg(hi, c_bf16) + (_dg(mid, c_bf16) + _dg(lo, c_bf16))


def _layer_norm(z, g, b, eps=1e-5):
    zc = z - jnp.mean(z, axis=-1, keepdims=True)
    var = jnp.mean(zc * zc, axis=-1, keepdims=True)
    return zc * lax.rsqrt(var + eps) * g + b


def _rms(x, g, eps=1e-6):
    return x * lax.rsqrt(jnp.mean(x * x, axis=-1, keepdims=True) + eps) * g


def _sigmoid(x):
    return 1.0 / (1.0 + jnp.exp(-x))


def _silu(x):
    return x * _sigmoid(x)


def _softplus(z):
    return jnp.maximum(z, 0.0) + jnp.log(1.0 + jnp.exp(-jnp.abs(z)))


def _gelu_tanh(x):
    c = math.sqrt(2.0 / math.pi)
    return x * (0.5 * (1.0 + jnp.tanh(c * (x + 0.044715 * (x * x * x)))))


def _params(sem, vmem=VMEM_LIMIT):
    return pltpu.CompilerParams(dimension_semantics=sem, vmem_limit_bytes=vmem)


def _full(shape):
    nd = len(shape)
    return pl.BlockSpec(shape, lambda *_: (0,) * nd, pipeline_mode=pl.Buffered(1))


def _rope_kernel(pos_ref, freq_ref, c_ref, s_ref):
    ang = pos_ref[...].astype(F32) * freq_ref[...]
    lane = lax.broadcasted_iota(jnp.int32, ang.shape, 1)
    c_ref[...] = jnp.where(lane < MLA_NOPE + MLA_ROPE, jnp.cos(ang), 0.0)
    s_ref[...] = jnp.sin(ang)


def _rope_tables(positions, tm):
    n = positions.size
    half = MLA_ROPE // 2
    inv_freq = ROPE_THETA ** (-jnp.arange(0, MLA_ROPE, 2, dtype=F32) / MLA_ROPE)
    freq = jnp.concatenate([jnp.zeros((MLA_NOPE,), F32), inv_freq, inv_freq,
                            jnp.zeros((LANE - MLA_NOPE - 2 * half,), F32)])[None, :]
    out = jax.ShapeDtypeStruct((n, LANE), F32)
    return pl.pallas_call(
        _rope_kernel,
        grid=(n // tm,),
        in_specs=[pl.BlockSpec((tm, 1), lambda i: (i, 0)), _full((1, LANE))],
        out_specs=[pl.BlockSpec((tm, LANE), lambda i: (i, 0))] * 2,
        out_shape=[out, out],
        compiler_params=_params(("parallel",)),
        name="rope_tables",
    )(positions.reshape(n, 1), freq)


IN_SPLITS = (MLA_PAD, RWKV_IN, S5_WIDTH, N_BRANCH * D_MODEL)
IN_PAD = sum(IN_SPLITS)


def _prep_w_in(w):
    cq = w[..., :Q_LORA]
    ckv = w[..., Q_LORA:Q_LORA + KV_LORA]
    kpe = w[..., Q_LORA + KV_LORA:Q_LORA + KV_LORA + MLA_ROPE]
    half = MLA_ROPE // 2
    k1, k2 = kpe[..., :half], kpe[..., half:]
    z64 = jnp.zeros(w.shape[:-1] + (MLA_NOPE,), w.dtype)
    z32 = jnp.zeros(w.shape[:-1] + (LANE - MLA_NOPE - MLA_ROPE,), w.dtype)
    rest = w[..., Q_LORA + KV_LORA + MLA_ROPE:]
    return jnp.concatenate([cq, ckv, z64, k1, k2, z32, z64, -k2, k1, z32, rest], axis=-1).astype(BF16)


def _in_proj_kernel(x_ref, w_ref, mla_ref, rwkv_ref, s5_ref, gate_ref):
    x = x_ref[...].astype(BF16)
    o0 = 0
    o1 = o0 + MLA_PAD
    o2 = o1 + RWKV_IN
    o3 = o2 + S5_WIDTH
    mla_ref[...] = _dg(x, w_ref[:, o0:o1])
    rwkv_ref[...] = _dg(x, w_ref[:, o1:o2])
    s5_ref[...] = _dg(x, w_ref[:, o2:o3])
    for n in range(N_BRANCH):
        c0 = o3 + n * D_MODEL
        gate_ref[:, n * D_MODEL:(n + 1) * D_MODEL] = _sigmoid(_dg(x, w_ref[:, c0:c0 + D_MODEL])).astype(BF16)


def _in_proj(x2, w, bsz, seq, tm):
    n = bsz * seq
    nt = seq // tm
    row = lambda b, i: (b * nt + i, 0)
    return pl.pallas_call(
        _in_proj_kernel,
        grid=(bsz, nt),
        in_specs=[pl.BlockSpec((tm, D_MODEL), row), _full((D_MODEL, IN_PAD))],
        out_specs=[pl.BlockSpec((tm, MLA_PAD), row),
                   pl.BlockSpec((tm, RWKV_IN), row),
                   pl.BlockSpec((tm, S5_WIDTH), lambda b, i: (i, b)),
                   pl.BlockSpec((tm, N_BRANCH * D_MODEL), row)],
        out_shape=[jax.ShapeDtypeStruct((n, MLA_PAD), F32),
                   jax.ShapeDtypeStruct((n, RWKV_IN), F32),
                   jax.ShapeDtypeStruct((seq, bsz * S5_WIDTH), F32),
                   jax.ShapeDtypeStruct((n, N_BRANCH * D_MODEL), BF16)],
        compiler_params=_params(("parallel", "parallel")),
        name="in_proj",
    )(x2, w)


def _prep_mla_weights(w_uq, w_ukv):
    half = MLA_ROPE // 2
    wq = w_uq.reshape(Q_LORA, MLA_HEADS, MLA_NOPE + MLA_ROPE)
    nope, pe1, pe2 = wq[..., :MLA_NOPE], wq[..., MLA_NOPE:MLA_NOPE + half], wq[..., MLA_NOPE + half:]
    z32 = jnp.zeros((Q_LORA, MLA_HEADS, LANE - MLA_NOPE - MLA_ROPE), w_uq.dtype)
    z64 = jnp.zeros((Q_LORA, MLA_HEADS, MLA_NOPE), w_uq.dtype)
    wq_main = jnp.concatenate([nope, pe1, pe2, z32], -1).reshape(Q_LORA, MLA_HEADS * LANE)
    wq_rot = jnp.concatenate([z64, -pe2, pe1, z32], -1).reshape(Q_LORA, MLA_HEADS * LANE)
    wkv = w_ukv.reshape(KV_LORA, MLA_HEADS, MLA_NOPE + MLA_V)
    k_nope, v = wkv[..., :MLA_NOPE], wkv[..., MLA_NOPE:]
    zk = jnp.zeros((KV_LORA, MLA_HEADS, LANE - MLA_NOPE), w_ukv.dtype)
    wk = jnp.concatenate([k_nope, zk], -1).reshape(KV_LORA, MLA_HEADS * LANE)
    zv = jnp.zeros_like(v)
    even = (jnp.arange(MLA_HEADS) % 2 == 0)[None, :, None]
    wv = jnp.concatenate([jnp.where(even, v, zv), jnp.where(even, zv, v)], -1).reshape(KV_LORA, MLA_HEADS * LANE)
    return wq_main.astype(BF16), wq_rot.astype(BF16), wk.astype(BF16), wv.astype(BF16)


def _mla_prep_kernel(p_ref, c_ref, s_ref, qn_ref, kvn_ref, wq_ref, wqr_ref, wk_ref, wv_ref,
                     q_out, k_out, v_out):
    cos = c_ref[...]
    sin = s_ref[...]
    qn = _rms(p_ref[:, :Q_LORA], qn_ref[...]).astype(BF16)
    kvn = _rms(p_ref[:, Q_LORA:Q_LORA + KV_LORA], kvn_ref[...]).astype(BF16)
    o = Q_LORA + KV_LORA
    k_rope = p_ref[:, o:o + LANE] * cos + p_ref[:, o + LANE:o + 2 * LANE] * sin
    scale = (MLA_NOPE + MLA_ROPE) ** -0.5 * math.log2(math.e)
    lane = lax.broadcasted_iota(jnp.int32, (1, LANE), 1)
    for h in range(MLA_HEADS):
        cols = slice(h * LANE, (h + 1) * LANE)
        q = _dg(qn, wq_ref[:, cols]) * cos + _dg(qn, wqr_ref[:, cols]) * sin
        q_out[0, h] = (q * scale).astype(BF16)
        k_out[0, h] = (_dg(kvn, wk_ref[:, cols]) + k_rope).astype(BF16)
        ones = jnp.where(lane == (1 - h % 2) * MLA_V, 1.0, 0.0)
        v_out[0, h] = (_dg(kvn, wv_ref[:, cols]) + ones).astype(BF16)


def _mla_prep(p_mla, cos, sin, q_norm, kv_norm, wq, wqr, wk, wv, bsz, seq, tm):
    nt = seq // tm
    row = lambda b, i: (b * nt + i, 0)
    hw = MLA_HEADS * LANE
    out = jax.ShapeDtypeStruct((bsz, MLA_HEADS, seq, LANE), BF16)
    ospec = pl.BlockSpec((1, MLA_HEADS, tm, LANE), lambda b, i: (b, 0, i, 0))
    return pl.pallas_call(
        _mla_prep_kernel,
        grid=(bsz, nt),
        in_specs=[pl.BlockSpec((tm, MLA_PAD), row),
                  pl.BlockSpec((tm, LANE), row), pl.BlockSpec((tm, LANE), row),
                  _full((1, Q_LORA)), _full((1, KV_LORA)),
                  _full((Q_LORA, hw)), _full((Q_LORA, hw)), _full((KV_LORA, hw)), _full((KV_LORA, hw))],
        out_specs=[ospec, ospec, ospec],
        out_shape=[out, out, out],
        compiler_params=_params(("parallel", "parallel")),
        name="mla_prep",
    )(p_mla, cos, sin, q_norm, kv_norm, wq, wqr, wk, wv)


def _flash_kernel(q_ref, k_ref, v_ref, o_ref, *, tq, td):
    i = pl.program_id(2)
    lane = lax.broadcasted_iota(jnp.int32, (tq, LANE), 1)
    qs = [q_ref[0, hh] for hh in range(2)]

    def scores(hh, start, size, r0=0):
        kj = k_ref[0, hh, pl.ds(start, size), :]
        vj = v_ref[0, hh, pl.ds(start, size), :]
        return _dg(qs[hh][r0:], kj, NT), vj

    def update(carry, s, vj):
        m, acc = carry
        m_new = jnp.maximum(m, jnp.max(s, axis=-1, keepdims=True))
        p = jnp.exp2(s - m_new)
        acc = jnp.exp2(m - m_new) * acc + _dg(p.astype(BF16), vj)
        return m_new, acc

    def body(j, carry):
        start = pl.multiple_of(j * tq, tq)
        return tuple(update(carry[hh], *scores(hh, start, tq)) for hh in range(2))

    init = (jnp.full((tq, 1), NEG_INF, F32), jnp.zeros((tq, LANE), F32))
    carry = lax.fori_loop(0, i, body, (init, init))
    for d in range(tq // td):
        r0 = d * td
        start = pl.multiple_of(i * tq + r0, td)
        causal = (lax.broadcasted_iota(jnp.int32, (tq - r0, td), 0)
                  >= lax.broadcasted_iota(jnp.int32, (tq - r0, td), 1))
        stepped = []
        for hh in range(2):
            m, acc = carry[hh]
            s, vj = scores(hh, start, td, r0)
            m_low, acc_low = update((m[r0:], acc[r0:]), jnp.where(causal, s, NEG_INF), vj)
            if r0:
                m_low = jnp.concatenate([m[:r0], m_low], axis=0)
                acc_low = jnp.concatenate([acc[:r0], acc_low], axis=0)
            stepped.append((m_low, acc_low))
        carry = tuple(stepped)
    out = None
    for hh in range(2):
        acc = carry[hh][1]
        ones_lane = (1 - hh) * MLA_V
        mine = (lane >= hh * MLA_V) & (lane < (hh + 1) * MLA_V)
        l = jnp.sum(jnp.where(lane == ones_lane, acc, 0.0), axis=-1, keepdims=True)
        o = jnp.where(mine, acc / l, 0.0)
        out = o if out is None else out + o
    o_ref[...] = out


def _flash(q, k, v, bsz, seq, tq, td):
    nt = seq // tq
    return pl.pallas_call(
        functools.partial(_flash_kernel, tq=tq, td=td),
        grid=(bsz, MLA_HEADS // 2, nt),
        in_specs=[pl.BlockSpec((1, 2, tq, LANE), lambda b, h, i: (b, h, i, 0)),
                  pl.BlockSpec((1, 2, seq, LANE), lambda b, h, i: (b, h, 0, 0)),
                  pl.BlockSpec((1, 2, seq, LANE), lambda b, h, i: (b, h, 0, 0))],
        out_specs=pl.BlockSpec((tq, LANE), lambda b, h, i: (b * nt + i, h)),
        out_shape=jax.ShapeDtypeStruct((bsz * seq, MLA_HEADS * MLA_V), F32),
        compiler_params=_params(("parallel", "parallel", "arbitrary")),
        name="mla_flash",
    )(q, k, v)


def _rwkv_prep_kernel(*refs, tm, has_vres):
    if has_vres:
        (p_ref, prev_ref, vf_ref, mu_ref, w0_ref, wup_ref, a0_ref, aup_ref, gup_ref, kk_ref, ka_ref,
         rk_ref, vd_ref, vu_ref, vb_ref, tri_ref, blk_ref, sel_ref, bd_ref,
         rh_o, ah_o, bh_o, kh_o, bt_o, kt_o, v_o, bon_o, g_o, gam_o) = refs
    else:
        (p_ref, prev_ref, mu_ref, w0_ref, wup_ref, a0_ref, aup_ref, gup_ref, kk_ref, ka_ref,
         rk_ref, tri_ref, blk_ref, sel_ref, bd_ref,
         rh_o, ah_o, bh_o, kh_o, bt_o, kt_o, v_o, bon_o, g_o, gam_o, vf_o) = refs
    i = pl.program_id(1)
    p = p_ref[...]
    first = jnp.where(i == 0, 0.0, prev_ref[7:8, :])
    rows = lax.broadcasted_iota(jnp.int32, p.shape, 0)
    prev = jnp.where(rows == 0, first, pltpu.roll(p, 1, axis=0))
    ps = p + (prev - p) * mu_ref[...]
    W = RWKV_WIDTH
    r, k, v = ps[:, :W], ps[:, W:2 * W], ps[:, 2 * W:3 * W]
    lora = ps[:, 3 * W:3 * W + DECAY_LORA + ICLR_LORA]
    g_d = ps[:, 3 * W + DECAY_LORA + ICLR_LORA:]
    w_log = -_softplus(-(w0_ref[...] + _dot(jnp.tanh(lora), wup_ref[...]))) - 0.5
    lw = -jnp.exp(w_log)
    iclr = _sigmoid(a0_ref[...] + _dot(lora, aup_ref[...]))
    g_o[...] = _dot(_sigmoid(g_d), gup_ref[...])
    if has_vres:
        mix = _sigmoid(vb_ref[...] + _dot(_dot(v, vd_ref[...]), vu_ref[...]))
        v = v + (vf_ref[...] - v) * mix
    bd = bd_ref[...]
    kk = k * kk_ref[...]
    kk = kk / jnp.maximum(jnp.sqrt(_dot_exact_rhs(kk * kk, bd)), 1e-12)
    k_h = k * (1.0 + (iclr - 1.0) * ka_ref[...])
    bon_o[...] = _dot_exact_rhs(r * k_h * rk_ref[...], bd) * v
    v_o[...] = v.astype(BF16)
    if not has_vres:
        vf_o[...] = v
    cum = _dot_exact_lhs(tri_ref[...], lw)
    tot = _dot_exact_lhs(blk_ref[...], lw)
    b = kk * iclr
    e_neg = jnp.exp(-cum)
    e_rem = jnp.exp(tot - cum)
    rh_o[...] = (r * jnp.exp(cum)).astype(BF16)
    ah_o[...] = (-kk * jnp.exp(cum - lw)).astype(BF16)
    bh_o[...] = (b * e_neg).astype(BF16)
    kh_o[...] = (k_h * e_neg).astype(BF16)
    bt_o[...] = (b * e_rem).astype(BF16)
    kt_o[...] = (k_h * e_rem).astype(BF16)
    gam_o[...] = jnp.exp(_dot_exact_lhs(sel_ref[...], lw))


def _rwkv_prep(p_rwkv, v_first, wts, consts, bsz, seq, tm):
    n = bsz * seq
    nt = seq // tm
    has_vres = v_first is not None
    row = lambda b, i: (b * nt + i, 0)
    prev_map = lambda b, i: (jnp.maximum((b * nt + i) * (tm // 8) - 1, 0), 0)
    wide = pl.BlockSpec((tm, RWKV_WIDTH), row)
    ins = [p_rwkv, p_rwkv]
    specs = [pl.BlockSpec((tm, RWKV_IN), row), pl.BlockSpec((8, RWKV_IN), prev_map)]
    if has_vres:
        ins.append(v_first)
        specs.append(wide)
    names = ["mu", "w0", "wup", "a0", "aup", "gup", "kk", "ka", "rk"]
    if has_vres:
        names += ["vd", "vu", "vb"]
    for nm in names:
        ins.append(wts[nm])
        specs.append(_full(wts[nm].shape))
    for c in consts:
        ins.append(c)
        specs.append(_full(c.shape))
    o = jax.ShapeDtypeStruct((n, RWKV_WIDTH), F32)
    ob = jax.ShapeDtypeStruct((n, RWKV_WIDTH), BF16)
    nch = tm // CHUNK
    extra = 0 if has_vres else 1
    return pl.pallas_call(
        functools.partial(_rwkv_prep_kernel, tm=tm, has_vres=has_vres),
        grid=(bsz, nt),
        in_specs=specs,
        out_specs=[wide] * 9 + [pl.BlockSpec((nch, RWKV_WIDTH), lambda b, i: (b * nt + i, 0))] + [wide] * extra,
        out_shape=[ob] * 7 + [o] * 2 + [jax.ShapeDtypeStruct((n // CHUNK, RWKV_WIDTH), F32)] + [o] * extra,
        compiler_params=_params(("parallel", "parallel")),
        name="rwkv_prep",
    )(*ins)


def _rwkv_scan_kernel(rh_ref, ah_ref, bh_ref, kh_ref, bt_ref, kt_ref, v_ref, bon_ref, g_ref, gam_ref,
                      lng_ref, lnb_ref, o_ref, s_ref, *, nch):
    c = pl.program_id(1)

    @pl.when(c == 0)
    def _():
        s_ref[...] = jnp.zeros_like(s_ref)

    C = CHUNK
    rr = lax.broadcasted_iota(jnp.int32, (C, C), 0)
    cc = lax.broadcasted_iota(jnp.int32, (C, C), 1)
    strict = rr > cc
    incl = rr >= cc
    eye = jnp.where(rr == cc, 1.0, 0.0).astype(F32)
    heads = range(RWKV_HEADS)
    sls = [slice(h * RWKV_HEAD, (h + 1) * RWKV_HEAD) for h in heads]
    pre = []
    for ci in range(nch):
        rows = slice(ci * C, (ci + 1) * C)
        ar = [jnp.concatenate([ah_ref[rows, sl], rh_ref[rows, sl]], axis=0) for sl in sls]
        bk = [jnp.concatenate([bh_ref[rows, sl], kh_ref[rows, sl]], axis=0) for sl in sls]
        gm = [_dot(ar[h], bk[h], NT) for h in heads]
        a_ab = [jnp.where(strict, g[:C, :C], 0.0) for g in gm]
        a_ak = [jnp.where(strict, g[:C, C:], 0.0) for g in gm]
        a_r = [jnp.concatenate([jnp.where(incl, g[C:, :C], 0.0), jnp.where(incl, g[C:, C:], 0.0)], axis=1)
               for g in gm]
        pre.append((ar, a_ab, a_ak, a_r))
    pw = [p[1] for p in pre]
    tinv = [[eye + a for a in p[1]] for p in pre]
    for _ in range(int(math.log2(C)) - 1):
        pw = [[_dot(p, p) for p in prow] for prow in pw]
        tinv = [[t + _dot(p, t) for p, t in zip(prow, trow)] for prow, trow in zip(pw, tinv)]
    s = [s_ref[h] for h in heads]
    for ci in range(nch):
        rows = slice(ci * C, (ci + 1) * C)
        ar, _, a_ak, a_r = pre[ci]
        gam = gam_ref[pl.ds((c * nch) % 8 + ci, 1), :]
        vv = [v_ref[rows, sl] for sl in sls]
        xs = [_dot(ar[h], s[h], NT) for h in heads]
        z = [xs[h][:C] + _dot(a_ak[h], vv[h]) for h in heads]
        u = [_dot(tinv[ci][h], z[h]) for h in heads]
        uv = [jnp.concatenate([u[h].astype(BF16), vv[h]], axis=0) for h in heads]
        o = [xs[h][C:] + _dot(a_r[h], uv[h]) for h in heads]
        btk = [jnp.concatenate([bt_ref[rows, sl], kt_ref[rows, sl]], axis=0) for sl in sls]
        s = [s[h] * gam[:, sls[h]] + _dot(uv[h], btk[h], TN) for h in heads]
        for h, sl in enumerate(sls):
            oc = o[h] - jnp.mean(o[h], axis=-1, keepdims=True)
            on = oc * lax.rsqrt(jnp.mean(oc * oc, axis=-1, keepdims=True) + RWKV_GN_EPS)
            y = on * lng_ref[:, sl] + lnb_ref[:, sl]
            o_ref[rows, sl] = (y + bon_ref[rows, sl]) * g_ref[rows, sl]
    for h in heads:
        s_ref[h] = s[h]


def _rwkv_scan(prep, ln_g, ln_b, bsz, seq, nch):
    n = bsz * seq
    tb = nch * CHUNK
    nc = seq // tb
    row = lambda b, c: (b * nc + c, 0)
    wide = pl.BlockSpec((tb, RWKV_WIDTH), row)
    return pl.pallas_call(
        functools.partial(_rwkv_scan_kernel, nch=nch),
        grid=(bsz, nc),
        in_specs=[wide] * 9 + [pl.BlockSpec((8, RWKV_WIDTH), lambda b, c: ((b * nc + c) * nch // 8, 0)),
                               _full((1, RWKV_WIDTH)), _full((1, RWKV_WIDTH))],
        out_specs=wide,
        out_shape=jax.ShapeDtypeStruct((n, RWKV_WIDTH), F32),
        scratch_shapes=[pltpu.VMEM((RWKV_HEADS, RWKV_HEAD, RWKV_HEAD), F32)],
        compiler_params=_params(("parallel", "arbitrary")),
        name="rwkv_scan",
    )(*prep, ln_g, ln_b)


def _rwkv_consts(tm):
    t = jnp.arange(tm)
    same = (t[:, None] // CHUNK) == (t[None, :] // CHUNK)
    tri = (same & (t[:, None] >= t[None, :])).astype(BF16)
    blk = same.astype(BF16)
    sel = ((jnp.arange(tm // CHUNK)[:, None]) == (t[None, :] // CHUNK)).astype(BF16)
    ch = jnp.arange(RWKV_WIDTH) // RWKV_HEAD
    bd = (ch[:, None] == ch[None, :]).astype(BF16)
    return tri, blk, sel, bd


def _prep_s5(a_re, a_im, log_dt, b_re, b_im, c_re, c_im, d, bsz):
    dt = jnp.exp(log_dt.astype(F32))[:, None]
    lam_re = jnp.minimum(a_re.astype(F32), -1e-4)
    lam_im = a_im.astype(F32)
    mag = jnp.exp(dt * lam_re)
    ab_re, ab_im = mag * jnp.cos(dt * lam_im), mag * jnp.sin(dt * lam_im)
    den = lam_re * lam_re + lam_im * lam_im
    f_re = ((ab_re - 1.0) * lam_re + ab_im * lam_im) / den
    f_im = (ab_im * lam_re - (ab_re - 1.0) * lam_im) / den
    bb_re = f_re[..., None] * b_re - f_im[..., None] * b_im
    bb_im = f_re[..., None] * b_im + f_im[..., None] * b_re
    eye = jnp.eye(S5_GROUPS, dtype=F32)

    def expand_b(bb):
        return jnp.einsum('gpc,gh->gchp', bb, eye).reshape(S5_WIDTH, S5_LANES)

    def expand_c(cm):
        return jnp.einsum('gcp,gh->gphc', cm.astype(F32), eye).reshape(S5_LANES, S5_WIDTH)

    tile = lambda a: jnp.broadcast_to(a.reshape(1, S5_LANES), (bsz, S5_LANES))
    return (expand_b(bb_re).astype(BF16), expand_b(bb_im).astype(BF16), tile(ab_re), tile(ab_im),
            expand_c(c_re).astype(BF16), expand_c(c_im).astype(BF16), d.astype(F32).reshape(1, S5_WIDTH))


def _s5_kernel(u_ref, bre_ref, bim_ref, are_ref, aim_ref, cre_ref, cim_ref, d_ref, wglu_ref, bglu_ref,
               o_ref, xre_s, xim_s, st_re, st_im, *, ts, bsz):
    @pl.when(pl.program_id(0) == 0)
    def _():
        st_re[...] = jnp.zeros_like(st_re)
        st_im[...] = jnp.zeros_like(st_im)

    u = u_ref[...]
    ub = u.astype(BF16)
    xre_s[...] = _dg(ub, bre_ref[...])
    xim_s[...] = _dg(ub, bim_ref[...])

    def step(t, carry):
        x_re, x_im = carry
        a_re = are_ref[...]
        a_im = aim_ref[...]
        r0 = pl.multiple_of(t * bsz, bsz)
        n_re = a_re * x_re - a_im * x_im + xre_s[pl.ds(r0, bsz), :]
        n_im = a_re * x_im + a_im * x_re + xim_s[pl.ds(r0, bsz), :]
        xre_s[pl.ds(r0, bsz), :] = n_re
        xim_s[pl.ds(r0, bsz), :] = n_im
        return n_re, n_im

    x_re, x_im = lax.fori_loop(0, ts, step, (st_re[...], st_im[...]))
    st_re[...] = x_re
    st_im[...] = x_im
    y = _dot(xre_s[...], cre_ref[...]) - _dot(xim_s[...], cim_ref[...]) + d_ref[...] * u
    y = _gelu_tanh(y)
    o_ref[...] = y * _sigmoid(_dot(y, wglu_ref[...]) + bglu_ref[...])


def _s5(u_tm, s5w, w_glu, b_glu, bsz, seq, ts):
    rows = ts * bsz
    bre, bim, are, aim, cre, cim, d = s5w
    tile = pl.BlockSpec((rows, S5_WIDTH), lambda i: (i, 0))
    return pl.pallas_call(
        functools.partial(_s5_kernel, ts=ts, bsz=bsz),
        grid=(seq // ts,),
        in_specs=[tile, _full(bre.shape), _full(bim.shape), _full(are.shape), _full(aim.shape),
                  _full(cre.shape), _full(cim.shape), _full(d.shape), _full(w_glu.shape), _full(b_glu.shape)],
        out_specs=tile,
        out_shape=jax.ShapeDtypeStruct((seq * bsz, S5_WIDTH), F32),
        scratch_shapes=[pltpu.VMEM((rows, S5_LANES), F32), pltpu.VMEM((rows, S5_LANES), F32),
                        pltpu.VMEM((bsz, S5_LANES), F32), pltpu.VMEM((bsz, S5_LANES), F32)],
        compiler_params=_params(("arbitrary",)),
        name="s5_scan",
    )(u_tm, bre, bim, are, aim, cre, cim, d, w_glu, b_glu)


def _merge_kernel(*refs, with_router):
    if with_router:
        (x_ref, ym_ref, yr_ref, ys_ref, gt_ref, wb_ref, wo_ref, g_ref, b_ref, rt_ref,
         o_ref, ob_ref, lg_ref) = refs
    else:
        x_ref, ym_ref, yr_ref, ys_ref, gt_ref, wb_ref, wo_ref, g_ref, b_ref, o_ref = refs
    mixed = None
    for n, y_ref in enumerate((ym_ref, yr_ref, ys_ref)):
        t = gt_ref[:, n * D_MODEL:(n + 1) * D_MODEL].astype(F32) * _dot(y_ref[...], wb_ref[n])
        mixed = t if mixed is None else mixed + t
    z = DEEPNORM_ALPHA * x_ref[...] + _dot(mixed, wo_ref[...])
    out = _layer_norm(z, g_ref[...], b_ref[...])
    o_ref[...] = out
    if with_router:
        ob_ref[...] = out.astype(BF16)
        lg_ref[...] = _dot3(out, rt_ref[...])


def _merge(x2, y_mla, y_rwkv, y_s5_tm, gates, wb, wo, g, b, router, bsz, seq, tm):
    n = bsz * seq
    nt = seq // tm
    row = lambda bb, i: (bb * nt + i, 0)
    with_router = router is not None
    ins = [x2, y_mla, y_rwkv, y_s5_tm, gates, wb, wo, g, b]
    specs = [pl.BlockSpec((tm, D_MODEL), row),
             pl.BlockSpec((tm, 512), row), pl.BlockSpec((tm, 512), row),
             pl.BlockSpec((tm, S5_WIDTH), lambda bb, i: (i, bb)),
             pl.BlockSpec((tm, N_BRANCH * D_MODEL), row),
             _full(wb.shape), _full(wo.shape), _full(g.shape), _full(b.shape)]
    out_specs = [pl.BlockSpec((tm, D_MODEL), row)]
    out_shape = [jax.ShapeDtypeStruct((n, D_MODEL), F32)]
    if with_router:
        ins.append(router)
        specs.append(_full(router.shape))
        out_specs += [pl.BlockSpec((tm, D_MODEL), row), pl.BlockSpec((tm, LANE), row)]
        out_shape += [jax.ShapeDtypeStruct((n, D_MODEL), BF16), jax.ShapeDtypeStruct((n, LANE), F32)]
    res = pl.pallas_call(
        functools.partial(_merge_kernel, with_router=with_router),
        grid=(bsz, nt),
        in_specs=specs,
        out_specs=out_specs,
        out_shape=out_shape,
        compiler_params=_params(("parallel", "parallel")),
        name="merge_ln1",
    )(*ins)
    return res if with_router else (res[0], None, None)


def _ffn_kernel(x_ref, wg_ref, wu_ref, wd_ref, g_ref, b_ref, o_ref, *, tf):
    x = x_ref[...]
    xb = x.astype(BF16)
    acc = None
    for c in range(FFN_DIM // tf):
        cols = slice(c * tf, (c + 1) * tf)
        h = _silu(_dg(xb, wg_ref[:, cols])) * _dg(xb, wu_ref[:, cols])
        part = _dot(h, wd_ref[cols, :])
        acc = part if acc is None else acc + part
    o_ref[...] = _layer_norm(DEEPNORM_ALPHA * x + acc, g_ref[...], b_ref[...])


def _ffn(x2, wg, wu, wd, g, b, tm, tf):
    n = x2.shape[0]
    tile = pl.BlockSpec((tm, D_MODEL), lambda i: (i, 0))
    return pl.pallas_call(
        functools.partial(_ffn_kernel, tf=tf),
        grid=(n // tm,),
        in_specs=[tile, _full(wg.shape), _full(wu.shape), _full(wd.shape), _full(g.shape), _full(b.shape)],
        out_specs=tile,
        out_shape=jax.ShapeDtypeStruct((n, D_MODEL), F32),
        compiler_params=_params(("parallel",)),
        name="ffn_ln2",
    )(x2, wg, wu, wd, g, b)


def _moe_kernel(be_ref, nu_ref, xs_ref, sw_ref, wg_ref, wu_ref, wd_ref, o_ref, acc_ref):
    i = pl.program_id(0)
    j = pl.program_id(1)

    @pl.when(i < nu_ref[0])
    def _():
        xb = xs_ref[...]
        h = _silu(_dg(xb, wg_ref[0])) * _dg(xb, wu_ref[0])
        part = _dot(h, wd_ref[0])

        @pl.when(j == 0)
        def _():
            acc_ref[...] = part

        @pl.when(j > 0)
        def _():
            acc_ref[...] += part

        @pl.when(j == pl.num_programs(1) - 1)
        def _():
            o_ref[...] = acc_ref[...] * sw_ref[...]

    @pl.when((i >= nu_ref[0]) & (j == pl.num_programs(1) - 1))
    def _():
        o_ref[...] = jnp.zeros_like(o_ref)


def _moe_experts(xs, slot_w, block_e, n_used, wg, wu, wd, tf):
    n_slots = xs.shape[0]
    n_blocks = n_slots // MOE_BLOCK
    nf = EXPERT_DIM // tf

    def blk(i, nu):
        return jnp.minimum(i, nu[0] - 1)

    def fcol(i, j, nu):
        return jnp.where(i < nu[0], j, nf - 1)

    grid_spec = pltpu.PrefetchScalarGridSpec(
        num_scalar_prefetch=2,
        grid=(n_blocks, nf),
        in_specs=[pl.BlockSpec((MOE_BLOCK, D_MODEL), lambda i, j, be, nu: (blk(i, nu), 0)),
                  pl.BlockSpec((MOE_BLOCK, 1), lambda i, j, be, nu: (blk(i, nu), 0)),
                  pl.BlockSpec((1, D_MODEL, tf), lambda i, j, be, nu: (be[blk(i, nu)], 0, fcol(i, j, nu))),
                  pl.BlockSpec((1, D_MODEL, tf), lambda i, j, be, nu: (be[blk(i, nu)], 0, fcol(i, j, nu))),
                  pl.BlockSpec((1, tf, D_MODEL), lambda i, j, be, nu: (be[blk(i, nu)], fcol(i, j, nu), 0))],
        out_specs=pl.BlockSpec((MOE_BLOCK, D_MODEL), lambda i, j, be, nu: (i, 0)),
        scratch_shapes=[pltpu.VMEM((MOE_BLOCK, D_MODEL), F32)],
    )
    return pl.pallas_call(
        _moe_kernel,
        grid_spec=grid_spec,
        out_shape=jax.ShapeDtypeStruct((n_slots, D_MODEL), F32),
        compiler_params=_params(("arbitrary", "arbitrary")),
        name="moe_experts",
    )(block_e, n_used, xs, slot_w, wg, wu, wd)


def _combine_kernel(x_ref, y0_ref, y1_ref, g_ref, b_ref, o_ref):
    z = DEEPNORM_ALPHA * x_ref[...] + (y0_ref[...] + y1_ref[...])
    o_ref[...] = _layer_norm(z, g_ref[...], b_ref[...])


def _combine(x2, y0, y1, g, b, tm):
    n = x2.shape[0]
    tile = pl.BlockSpec((tm, D_MODEL), lambda i: (i, 0))
    return pl.pallas_call(
        _combine_kernel,
        grid=(n // tm,),
        in_specs=[tile, tile, tile, _full(g.shape), _full(b.shape)],
        out_specs=tile,
        out_shape=jax.ShapeDtypeStruct((n, D_MODEL), F32),
        compiler_params=_params(("parallel",)),
        name="moe_combine_ln2",
    )(x2, y0, y1, g, b)


def _moe(h2, h2b, logits, wg, wu, wd, e_off, g, b, tm, tf):
    n_tok = h2.shape[0]
    n_pair = n_tok * TOP_K
    top_logit, top_e = lax.top_k(logits[:, :N_EXPERTS], TOP_K)
    top_w = jax.nn.softmax(top_logit, axis=-1)
    pair_e = top_e.reshape(-1).astype(jnp.int32)
    onehot = (pair_e[:, None] == jnp.arange(N_EXPERTS, dtype=jnp.int32)[None, :]).astype(jnp.int32)
    csum = jnp.cumsum(onehot, axis=0)
    counts = csum[-1]
    start = jnp.cumsum(counts) - counts
    padded = (counts + MOE_BLOCK - 1) // MOE_BLOCK * MOE_BLOCK
    pad_end = jnp.cumsum(padded)
    pad_start = pad_end - padded
    pair_slot = jnp.sum(onehot * (pad_start[None, :] + csum - 1), axis=1)
    n_blocks = n_pair // MOE_BLOCK + N_EXPERTS
    n_slots = n_blocks * MOE_BLOCK
    block_e = jnp.minimum(jnp.searchsorted(pad_end, jnp.arange(n_blocks, dtype=jnp.int32) * MOE_BLOCK,
                                           side='right'), N_EXPERTS - 1).astype(jnp.int32)
    n_used = (pad_end[-1] // MOE_BLOCK).astype(jnp.int32).reshape(1)
    pair_tok = jnp.arange(n_pair, dtype=jnp.int32) // TOP_K
    _, tok_sorted, w_sorted = lax.sort((pair_e, pair_tok, top_w.reshape(-1)), num_keys=1, is_stable=True)
    tail = n_slots - n_pair
    tok_pad = jnp.concatenate([tok_sorted, jnp.zeros((tail,), jnp.int32)])
    w_pad = jnp.concatenate([w_sorted, jnp.zeros((tail,), F32)])
    slot = jnp.arange(n_slots, dtype=jnp.int32)
    slot_e = jnp.repeat(block_e, MOE_BLOCK)
    valid = slot < jnp.repeat((pad_start + counts)[block_e], MOE_BLOCK)
    shift = pad_start - start
    slot_tok = jnp.zeros((n_slots,), jnp.int32)
    slot_w = jnp.zeros((n_slots,), F32)
    for e in range(N_EXPERTS):
        here = valid & (slot_e == e)
        slot_tok = jnp.where(here, jnp.roll(tok_pad, shift[e]), slot_tok)
        slot_w = jnp.where(here, jnp.roll(w_pad, shift[e]), slot_w)
    xs = jnp.take(h2b, slot_tok, axis=0, mode="clip")
    y = _moe_experts(xs, slot_w[:, None], block_e + e_off, n_used, wg, wu, wd, tf)
    ps = pair_slot.reshape(n_tok, TOP_K)
    y0 = jnp.take(y, ps[:, 0], axis=0, mode="clip")
    y1 = jnp.take(y, ps[:, 1], axis=0, mode="clip")
    return _combine(h2, y0, y1, g, b, tm)


def kernel(x, positions, w_in, mla_q_norm, mla_w_uq, mla_kv_norm, mla_w_ukv,
           rwkv_mu, rwkv_w0, rwkv_w_up, rwkv_a0, rwkv_a_up, rwkv_g_up, rwkv_k_k, rwkv_k_a,
           rwkv_r_k, rwkv_ln_g, rwkv_ln_b, rwkv_vres_down, rwkv_vres_up, rwkv_vres_b,
           s5_a_re, s5_a_im, s5_log_dt, s5_b_re, s5_b_im, s5_c_re, s5_c_im, s5_d, s5_w_glu, s5_b_glu,
           w_branch, w_out, ln1_g, ln1_b, ln2_g, ln2_b,
           ffn_w_gate, ffn_w_up, ffn_w_down, moe_router, moe_w_gate, moe_w_up, moe_w_down):
    bsz, seq, _ = x.shape
    n = bsz * seq
    depth = w_in.shape[0]
    tm = min(512, seq)
    tq = min(FLASH_TQ, seq)
    ts = min(64, seq)
    row = lambda a: a.reshape(1, -1)

    cos, sin = _rope_tables(positions, tm)
    w_in_p = _prep_w_in(w_in)
    consts = _rwkv_consts(tm)
    zpad = jnp.zeros((DECAY_LORA, RWKV_WIDTH), F32)
    stack = lambda w: w.astype(BF16).reshape((-1,) + w.shape[2:])
    moe_wg, moe_wu, moe_wd = stack(moe_w_gate), stack(moe_w_up), stack(moe_w_down)

    h = x.reshape(n, D_MODEL)
    v_first = None
    for l in range(depth):
        p_mla, p_rwkv, u_s5, gates = _in_proj(h, w_in_p[l], bsz, seq, tm)

        wq, wqr, wk, wv = _prep_mla_weights(mla_w_uq[l], mla_w_ukv[l])
        q, k, v = _mla_prep(p_mla, cos, sin, row(mla_q_norm[l]), row(mla_kv_norm[l]),
                            wq, wqr, wk, wv, bsz, seq, tm)
        y_mla = _flash(q, k, v, bsz, seq, tq, min(FLASH_TD, tq))

        wts = {
            "mu": row(rwkv_mu[l]), "w0": row(rwkv_w0[l]),
            "wup": jnp.concatenate([rwkv_w_up[l], zpad], 0).astype(BF16),
            "a0": row(rwkv_a0[l]),
            "aup": jnp.concatenate([zpad, rwkv_a_up[l]], 0).astype(BF16),
            "gup": rwkv_g_up[l].astype(BF16),
            "kk": row(rwkv_k_k[l]), "ka": row(rwkv_k_a[l]), "rk": row(rwkv_r_k[l]),
        }
        if l > 0:
            wts["vd"] = rwkv_vres_down[l - 1].astype(BF16)
            wts["vu"] = rwkv_vres_up[l - 1].astype(BF16)
            wts["vb"] = row(rwkv_vres_b[l - 1])
        prep = _rwkv_prep(p_rwkv, v_first, wts, consts, bsz, seq, tm)
        if l == 0:
            v_first = prep[10]
        prep = prep[:10]
        y_rwkv = _rwkv_scan(prep, row(rwkv_ln_g[l]), row(rwkv_ln_b[l]), bsz, seq, RWKV_NCH)

        s5w = _prep_s5(s5_a_re[l], s5_a_im[l], s5_log_dt[l], s5_b_re[l], s5_b_im[l],
                       s5_c_re[l], s5_c_im[l], s5_d[l], bsz)
        y_s5 = _s5(u_s5.reshape(seq * bsz, S5_WIDTH), s5w, s5_w_glu[l].astype(BF16), row(s5_b_glu[l]),
                   bsz, seq, ts)

        moe_layer = l % 2 == 1
        router = None
        if moe_layer:
            router = jnp.pad(moe_router[l // 2], ((0, 0), (0, LANE - N_EXPERTS)))
        h, hb, logits = _merge(h, y_mla, y_rwkv, y_s5.reshape(seq, bsz * S5_WIDTH), gates,
                               w_branch[l].astype(BF16), w_out[l].astype(BF16), row(ln1_g[l]), row(ln1_b[l]),
                               router, bsz, seq, tm)
        if moe_layer:
            h = _moe(h, hb, logits, moe_wg, moe_wu, moe_wd, N_EXPERTS * (l // 2),
                     row(ln2_g[l]), row(ln2_b[l]), tm, tf=1792)
        else:
            h = _ffn(h, ffn_w_gate[l // 2].astype(BF16), ffn_w_up[l // 2].astype(BF16),
                     ffn_w_down[l // 2].astype(BF16), row(ln2_g[l]), row(ln2_b[l]), tm, tf=1408)
    return h.reshape(bsz, seq, D_MODEL)
```
